```python
import math
import jax, jax.numpy as jnp
from jax import lax
import numpy as np


D_MODEL = 2048
BATCH = 2
SEQ = 4096
DEPTH = 2
DEC_BATCH = 8
DEC_SEQ = 4
PAST_LEN = 16384
PAGE_SIZE = 128

N_HEADS = 16
HEAD_DIM = D_MODEL // N_HEADS
GROUP_WINDOWS = (128, 512, 2048)
GROUP_DILATIONS = (1, 4, 16)
GROUP_HEADS = (6, 5, 5)
N_GROUPS = len(GROUP_HEADS)
HEAD_OFFSETS = tuple(int(o) for o in np.cumsum((0,) + GROUP_HEADS))
CONV_WIDTH = 3
D_FF = ((8 * D_MODEL // 3 + 255) // 256) * 256
N_EXPERTS = 8
TOP_K = 2
D_FF_EXPERT = 7 * D_MODEL // 2
RMS_EPS = 1e-6
NEG_INF = -1e30
N_CONV_LAYERS = (DEPTH + 1) // 2
N_ATTN_LAYERS = DEPTH // 2

kernel_name = 'hybrid_shortconv_dilated_swa_moe_step'


def rms_norm(x, w):
    xf = x.astype(jnp.float32)
    y = xf * lax.rsqrt(jnp.mean(xf * xf, axis=-1, keepdims=True) + RMS_EPS)
    return (y * w.astype(jnp.float32)).astype(x.dtype)


def swiglu(h, gate_up_w, down_w):
    g, u = jnp.split(h @ gate_up_w, 2, axis=-1)
    return (jax.nn.silu(g) * u) @ down_w


def moe_ffn(h, router_w, gate_up_w, down_w):
    probs = jax.nn.softmax((h @ router_w).astype(jnp.float32), axis=-1)
    top_p, top_i = lax.top_k(probs, TOP_K)
    top_p = top_p / jnp.sum(top_p, axis=-1, keepdims=True)
    gates = jnp.sum(jax.nn.one_hot(top_i, N_EXPERTS, dtype=jnp.float32) * top_p[..., None], axis=-2)
    gates = gates.astype(h.dtype)
    out = jnp.zeros_like(h)
    for e in range(N_EXPERTS):
        out = out + gates[..., e:e + 1] * swiglu(h, gate_up_w[e], down_w[e])
    return out


def depthwise_causal_conv(u_ext, conv_w):
    return lax.conv_general_dilated(u_ext, conv_w[:, None, :].astype(u_ext.dtype), window_strides=(1,),
                                    padding='VALID', dimension_numbers=('NWC', 'WIO', 'NWC'),
                                    feature_group_count=u_ext.shape[-1])


def conv_inputs(h, in_w):
    b, c, hx = jnp.split(h @ in_w, 3, axis=-1)
    return b, c * hx


def conv_mixer_prompt(h, in_w, conv_w, out_w):
    b, u = conv_inputs(h, in_w)
    u_ext = jnp.pad(u, ((0, 0), (CONV_WIDTH - 1, 0), (0, 0)))
    y = (b * depthwise_causal_conv(u_ext, conv_w)) @ out_w
    return y, u[:, -(CONV_WIDTH - 1):]


def conv_mixer_sample(h, conv_state, in_w, conv_w, out_w):
    b, u = conv_inputs(h, in_w)
    u_ext = jnp.concatenate([conv_state.astype(u.dtype), u], axis=1)
    y = (b * depthwise_causal_conv(u_ext, conv_w)) @ out_w
    return y, u_ext[:, -(CONV_WIDTH - 1):]


def qkv_project(h, qkv_w, q_norm_w, k_norm_w):
    qkv = (h @ qkv_w).reshape(h.shape[:-1] + (3, N_HEADS, HEAD_DIM))
    q = rms_norm(qkv[..., 0, :, :], q_norm_w)
    k = rms_norm(qkv[..., 1, :, :], k_norm_w)
    return q, k, qkv[..., 2, :, :]


def softmax_with_lse(s):
    m = jnp.max(s, axis=-1, keepdims=True)
    e = jnp.exp(s - m)
    den = jnp.sum(e, axis=-1, keepdims=True)
    return e / den, (m + jnp.log(den))[..., 0]


def dilated_group_prompt(q, k, v, window, dilation):
    b, s, hg, hd = q.shape
    blk = window // dilation
    span = dilation * blk
    s_pad = -(-s // span) * span
    m_len = s_pad // dilation
    nb = m_len // blk

    def to_streams(t):
        t = jnp.pad(t, ((0, 0), (0, s_pad - s), (0, 0), (0, 0)))
        t = t.reshape(b, m_len, dilation, hg, hd).transpose(0, 2, 1, 3, 4)
        return t.reshape(b, dilation, nb, blk, hg, hd)

    def with_prev(t):
        prev = jnp.pad(t, ((0, 0), (0, 0), (1, 0), (0, 0), (0, 0), (0, 0)))[:, :, :-1]
        return jnp.concatenate([prev, t], axis=3)

    qs = to_streams(q)
    ks = with_prev(to_streams(k))
    vs = with_prev(to_streams(v))
    scores = jnp.einsum('brnqhe,brnkhe->brnhqk', qs, ks).astype(jnp.float32) * (HEAD_DIM ** -0.5)
    qi = jnp.arange(blk)[:, None]
    ki = jnp.arange(2 * blk)[None, :]
    band = (ki >= qi) & (ki <= qi + blk)
    has_prev = (jnp.arange(nb)[:, None, None] > 0) | (ki >= blk)[None]
    mask = band[None] & has_prev
    scores = jnp.where(mask[None, None, :, None], scores, NEG_INF)
    p, lse = softmax_with_lse(scores)
    o = jnp.einsum('brnhqk,brnkhe->brnqhe', p.astype(v.dtype), vs)
    o = o.reshape(b, dilation, m_len, hg, hd).transpose(0, 2, 1, 3, 4).reshape(b, s_pad, hg, hd)[:, :s]
    lse = lse.transpose(0, 1, 2, 4, 3).reshape(b, dilation, m_len, hg).transpose(0, 2, 1, 3)
    return o, lse.reshape(b, s_pad, hg)[:, :s]


def dilated_group_sample(q, kv_new, cache_kv, window, dilation):
    t_new = q.shape[1]
    past = cache_kv.shape[1]
    ext = jnp.concatenate([cache_kv.astype(kv_new.dtype), kv_new], axis=1)
    n_keys = window // dilation + 1
    idx = past + jnp.arange(t_new)[:, None] - dilation * jnp.arange(n_keys)[None, :]
    valid = idx >= 0
    g = ext[:, jnp.maximum(idx, 0)]
    scores = jnp.einsum('bthe,btjhe->bhtj', q, g[:, :, :, 0]).astype(jnp.float32) * (HEAD_DIM ** -0.5)
    scores = jnp.where(valid[None, None], scores, NEG_INF)
    p, lse = softmax_with_lse(scores)
    o = jnp.einsum('bhtj,btjhe->bthe', p.astype(q.dtype), g[:, :, :, 1])
    return o, lse.transpose(0, 2, 1), ext[:, t_new:]


def merge_groups(outs, lses, out_w):
    log_den = jnp.stack([jax.nn.logsumexp(l, axis=-1) - math.log(n) for l, n in zip(lses, GROUP_HEADS)], axis=-1)
    alpha = N_GROUPS * jax.nn.softmax(log_den, axis=-1)
    merged = jnp.concatenate([o * alpha[..., gi, None, None].astype(o.dtype) for gi, o in enumerate(outs)], axis=-2)
    return merged.reshape(merged.shape[:-2] + (N_HEADS * HEAD_DIM,)) @ out_w


def attention_prompt(h, qkv_w, q_norm_w, k_norm_w, out_w):
    q, k, v = qkv_project(h, qkv_w, q_norm_w, k_norm_w)
    outs, lses, states = [], [], []
    for gi in range(N_GROUPS):
        lo, hi = HEAD_OFFSETS[gi], HEAD_OFFSETS[gi + 1]
        o, lse = dilated_group_prompt(q[:, :, lo:hi], k[:, :, lo:hi], v[:, :, lo:hi],
                                      GROUP_WINDOWS[gi], GROUP_DILATIONS[gi])
        keep = min(GROUP_WINDOWS[gi], h.shape[1])
        states.append(jnp.stack([k[:, -keep:, lo:hi], v[:, -keep:, lo:hi]], axis=2))
        outs.append(o)
        lses.append(lse)
    return merge_groups(outs, lses, out_w), states


def attention_sample(h, caches, qkv_w, q_norm_w, k_norm_w, out_w):
    q, k, v = qkv_project(h, qkv_w, q_norm_w, k_norm_w)
    outs, lses, states = [], [], []
    for gi in range(N_GROUPS):
        lo, hi = HEAD_OFFSETS[gi], HEAD_OFFSETS[gi + 1]
        kv_new = jnp.stack([k[:, :, lo:hi], v[:, :, lo:hi]], axis=2)
        o, lse, st = dilated_group_sample(q[:, :, lo:hi], kv_new, caches[gi],
                                          GROUP_WINDOWS[gi], GROUP_DILATIONS[gi])
        outs.append(o)
        lses.append(lse)
        states.append(st)
    return merge_groups(outs, lses, out_w), states


def setup_inputs(seed: int = 0) -> dict:
    key = jax.random.key(seed)
    ks = jax.random.split(key, 24)
    f32 = jnp.float32

    def dense(k, shape, fan_in):
        return jax.random.normal(k, shape, f32) * (fan_in ** -0.5)

    def gain(k, shape):
        return 1.0 + 0.1 * jax.random.normal(k, shape, f32)

    kv_lens = [min(w, PAST_LEN) for w in GROUP_WINDOWS]
    return {
        'x_prompt': jax.random.normal(ks[0], (BATCH, SEQ, D_MODEL), f32),
        'x_sample': jax.random.normal(ks[1], (DEC_BATCH, DEC_SEQ, D_MODEL), f32),
        'state_conv': jax.random.normal(ks[2], (N_CONV_LAYERS, DEC_BATCH, CONV_WIDTH - 1, D_MODEL), f32),
        'cache_kv_w128': jax.random.normal(ks[3], (N_ATTN_LAYERS, DEC_BATCH, kv_lens[0], 2, GROUP_HEADS[0], HEAD_DIM), f32),
        'cache_kv_w512': jax.random.normal(ks[4], (N_ATTN_LAYERS, DEC_BATCH, kv_lens[1], 2, GROUP_HEADS[1], HEAD_DIM), f32),
        'cache_kv_w2048': jax.random.normal(ks[5], (N_ATTN_LAYERS, DEC_BATCH, kv_lens[2], 2, GROUP_HEADS[2], HEAD_DIM), f32),
        'norm_mix_w': gain(ks[6], (DEPTH, D_MODEL)),
        'norm_ffn_w': gain(ks[7], (DEPTH, D_MODEL)),
        'conv_in_w': dense(ks[8], (N_CONV_LAYERS, D_MODEL, 3 * D_MODEL), D_MODEL),
        'conv_w': dense(ks[9], (N_CONV_LAYERS, CONV_WIDTH, D_MODEL), CONV_WIDTH),
        'conv_out_w': dense(ks[10], (N_CONV_LAYERS, D_MODEL, D_MODEL), D_MODEL),
        'attn_qkv_w': dense(ks[11], (N_ATTN_LAYERS, D_MODEL, 3 * N_HEADS * HEAD_DIM), D_MODEL),
        'q_norm_w': gain(ks[12], (N_ATTN_LAYERS, HEAD_DIM)),
        'k_norm_w': gain(ks[13], (N_ATTN_LAYERS, HEAD_DIM)),
        'attn_out_w': dense(ks[14], (N_ATTN_LAYERS, N_HEADS * HEAD_DIM, D_MODEL), N_HEADS * HEAD_DIM),
        'ffn_gate_up_w': dense(ks[15], (N_CONV_LAYERS, D_MODEL, 2 * D_FF), D_MODEL),
        'ffn_down_w': dense(ks[16], (N_CONV_LAYERS, D_FF, D_MODEL), D_FF),
        'router_w': dense(ks[17], (N_ATTN_LAYERS, D_MODEL, N_EXPERTS), D_MODEL),
        'moe_gate_up_w': dense(ks[18], (N_ATTN_LAYERS, N_EXPERTS, D_MODEL, 2 * D_FF_EXPERT), D_MODEL),
        'moe_down_w': dense(ks[19], (N_ATTN_LAYERS, N_EXPERTS, D_FF_EXPERT, D_MODEL), D_FF_EXPERT),
    }


def reference(x_prompt, x_sample, state_conv, cache_kv_w128, cache_kv_w512, cache_kv_w2048,
              norm_mix_w, norm_ffn_w, conv_in_w, conv_w, conv_out_w, attn_qkv_w, q_norm_w, k_norm_w,
              attn_out_w, ffn_gate_up_w, ffn_down_w, router_w, moe_gate_up_w, moe_down_w):
    xp, xs = x_prompt, x_sample
    conv_p, conv_s = [], []
    kv_p = [[] for _ in range(N_GROUPS)]
    kv_s = [[] for _ in range(N_GROUPS)]
    for i in range(DEPTH):
        j = i // 2
        hp = rms_norm(xp, norm_mix_w[i])
        hs = rms_norm(xs, norm_mix_w[i])
        if i % 2 == 0:
            dp, sp = conv_mixer_prompt(hp, conv_in_w[j], conv_w[j], conv_out_w[j])
            ds, ss = conv_mixer_sample(hs, state_conv[j], conv_in_w[j], conv_w[j], conv_out_w[j])
            conv_p.append(sp)
            conv_s.append(ss)
        else:
            dp, sp = attention_prompt(hp, attn_qkv_w[j], q_norm_w[j], k_norm_w[j], attn_out_w[j])
            caches = (cache_kv_w128[j], cache_kv_w512[j], cache_kv_w2048[j])
            ds, ss = attention_sample(hs, caches, attn_qkv_w[j], q_norm_w[j], k_norm_w[j], attn_out_w[j])
            for gi in range(N_GROUPS):
                kv_p[gi].append(sp[gi])
                kv_s[gi].append(ss[gi])
        xp = xp + dp
        xs = xs + ds
        hp = rms_norm(xp, norm_ffn_w[i])
        hs = rms_norm(xs, norm_ffn_w[i])
        if i % 2 == 0:
            xp = xp + swiglu(hp, ffn_gate_up_w[j], ffn_down_w[j])
            xs = xs + swiglu(hs, ffn_gate_up_w[j], ffn_down_w[j])
        else:
            xp = xp + moe_ffn(hp, router_w[j], moe_gate_up_w[j], moe_down_w[j])
            xs = xs + moe_ffn(hs, router_w[j], moe_gate_up_w[j], moe_down_w[j])
    y_prompt, y_sample = xp, xs
    new_conv_prompt = jnp.stack(conv_p)
    new_kv_w128_prompt = jnp.stack(kv_p[0])
    new_kv_w512_prompt = jnp.stack(kv_p[1])
    new_kv_w2048_prompt = jnp.stack(kv_p[2])
    new_conv_sample = jnp.stack(conv_s)
    new_kv_w128_sample = jnp.stack(kv_s[0])
    new_kv_w512_sample = jnp.stack(kv_s[1])
    new_kv_w2048_sample = jnp.stack(kv_s[2])
    return (y_prompt, y_sample, new_conv_prompt, new_kv_w128_prompt, new_kv_w512_prompt, new_kv_w2048_prompt,
            new_conv_sample, new_kv_w128_sample, new_kv_w512_sample, new_kv_w2048_sample)
```

```python
import functools
import math

import jax
import jax.numpy as jnp
from jax import lax
from jax.experimental import pallas as pl
from jax.experimental.pallas import tpu as pltpu

D_MODEL = 2048
N_HEADS = 16
HEAD_DIM = 128
GROUP_WINDOWS = (128, 512, 2048)
GROUP_DILATIONS = (1, 4, 16)
GROUP_HEADS = (6, 5, 5)
HEAD_OFFSETS = (0, 6, 11, 16)
N_GROUPS = 3
CONV_WIDTH = 3
N_EXPERTS = 8
TOP_K = 2
RMS_EPS = 1e-6
NEG_INF = -1e30

LANES = 128
SUBLANES = 8
ATTN_BLOCK = 128
ATTN_TILE = 2048
VMEM_LIMIT = 56 * 1024 * 1024

BF16 = jnp.bfloat16
F32 = jnp.float32


def _params(*sem):
    return pltpu.CompilerParams(dimension_semantics=sem, vmem_limit_bytes=VMEM_LIMIT)


def _dot(a, b):
    return jnp.dot(a, b, preferred_element_type=F32)


def _dot_nt(a, b):
    return lax.dot_general(a, b, (((1,), (1,)), ((), ())), preferred_element_type=F32)


def _rmsnorm_rows(x, gain):
    ms = jnp.mean(x * x, axis=-1, keepdims=True)
    return (x * lax.rsqrt(ms + RMS_EPS)) * gain


def _conv_in_kernel(x_ref, g_ref, wb_ref, wc_ref, wh_ref, b_ref, u_ref, h_scr):
    @pl.when(pl.program_id(1) == 0)
    def _():
        h_scr[...] = _rmsnorm_rows(x_ref[...], g_ref[...]).astype(BF16)

    h = h_scr[...]
    b_ref[...] = _dot(h, wb_ref[...].astype(BF16))
    c = _dot(h, wc_ref[...].astype(BF16))
    hx = _dot(h, wh_ref[...].astype(BF16))
    u_ref[...] = c * hx


def conv_in_proj(x, gain, w, *, tm, tn):
    m, d = x.shape
    nj = d // tn
    return pl.pallas_call(
        _conv_in_kernel,
        out_shape=(jax.ShapeDtypeStruct((m, d), F32), jax.ShapeDtypeStruct((m, d), F32)),
        grid=(m // tm, nj),
        in_specs=[
            pl.BlockSpec((tm, d), lambda i, j: (i, 0)),
            pl.BlockSpec((1, d), lambda i, j: (0, 0)),
            pl.BlockSpec((d, tn), lambda i, j: (0, j)),
            pl.BlockSpec((d, tn), lambda i, j: (0, j + nj)),
            pl.BlockSpec((d, tn), lambda i, j: (0, j + 2 * nj)),
        ],
        out_specs=(pl.BlockSpec((tm, tn), lambda i, j: (i, j)),
                   pl.BlockSpec((tm, tn), lambda i, j: (i, j))),
        scratch_shapes=[pltpu.VMEM((tm, d), BF16)],
        compiler_params=_params("parallel", "arbitrary"),
        name="conv_in_proj",
    )(x, gain.reshape(1, d), w, w, w)


def _ffn_up_kernel(x_ref, g_ref, wg_ref, wu_ref, a_ref, h_scr):
    @pl.when(pl.program_id(1) == 0)
    def _():
        h_scr[...] = _rmsnorm_rows(x_ref[...], g_ref[...]).astype(BF16)

    h = h_scr[...]
    g = _dot(h, wg_ref[...].astype(BF16))
    u = _dot(h, wu_ref[...].astype(BF16))
    a_ref[...] = (g * jax.nn.sigmoid(g) * u).astype(BF16)


def ffn_up(x, gain, w, *, tm, tn):
    m, d = x.shape
    f = w.shape[1] // 2
    nj = f // tn
    return pl.pallas_call(
        _ffn_up_kernel,
        out_shape=jax.ShapeDtypeStruct((m, f), BF16),
        grid=(m // tm, nj),
        in_specs=[
            pl.BlockSpec((tm, d), lambda i, j: (i, 0)),
            pl.BlockSpec((1, d), lambda i, j: (0, 0)),
            pl.BlockSpec((d, tn), lambda i, j: (0, j)),
            pl.BlockSpec((d, tn), lambda i, j: (0, j + nj)),
        ],
        out_specs=pl.BlockSpec((tm, tn), lambda i, j: (i, j)),
        scratch_shapes=[pltpu.VMEM((tm, d), BF16)],
        compiler_params=_params("parallel", "arbitrary"),
        name="ffn_up",
    )(x, gain.reshape(1, d), w, w)


def _qkv_kernel(x_ref, g_ref, w_ref, qn_ref, kn_ref, o_ref, h_scr, *, tn, d_model):
    j = pl.program_id(1)

    @pl.when(j == 0)
    def _():
        h_scr[...] = _rmsnorm_rows(x_ref[...], g_ref[...]).astype(BF16)

    o_ref[...] = _dot(h_scr[...], w_ref[...].astype(BF16))
    nq = d_model // tn

    @pl.when(j < 2 * nq)
    def _():
        gain = jnp.where(j < nq, qn_ref[...], kn_ref[...])
        for hh in range(tn // HEAD_DIM):
            cols = slice(hh * HEAD_DIM, (hh + 1) * HEAD_DIM)
            o_ref[:, cols] = _rmsnorm_rows(o_ref[:, cols], gain)


def qkv_proj(x, gain, w, q_norm_w, k_norm_w, *, tm, tn):
    m, d = x.shape
    n = w.shape[1]
    return pl.pallas_call(
        functools.partial(_qkv_kernel, tn=tn, d_model=d),
        out_shape=jax.ShapeDtypeStruct((m, n), F32),
        grid=(m // tm, n // tn),
        in_specs=[
            pl.BlockSpec((tm, d), lambda i, j: (i, 0)),
            pl.BlockSpec((1, d), lambda i, j: (0, 0)),
            pl.BlockSpec((d, tn), lambda i, j: (0, j)),
            pl.BlockSpec((1, HEAD_DIM), lambda i, j: (0, 0)),
            pl.BlockSpec((1, HEAD_DIM), lambda i, j: (0, 0)),
        ],
        out_specs=pl.BlockSpec((tm, tn), lambda i, j: (i, j)),
        scratch_shapes=[pltpu.VMEM((tm, d), BF16)],
        compiler_params=_params("parallel", "arbitrary"),
        name="qkv_proj",
    )(x, gain.reshape(1, d), w, q_norm_w.reshape(1, HEAD_DIM), k_norm_w.reshape(1, HEAD_DIM))


def _matmul_res_kernel(a_ref, w_ref, r_ref, o_ref):
    o_ref[...] = r_ref[...] + _dot(a_ref[...], w_ref[...].astype(BF16))


def matmul_res(a, w, res, *, tm, tn):
    m, k = a.shape
    n = w.shape[1]
    return pl.pallas_call(
        _matmul_res_kernel,
        out_shape=jax.ShapeDtypeStruct((m, n), F32),
        grid=(m // tm, n // tn),
        in_specs=[
            pl.BlockSpec((tm, k), lambda i, j: (i, 0)),
            pl.BlockSpec((k, tn), lambda i, j: (0, j)),
            pl.BlockSpec((tm, tn), lambda i, j: (i, j)),
        ],
        out_specs=pl.BlockSpec((tm, tn), lambda i, j: (i, j)),
        compiler_params=_params("parallel", "arbitrary"),
        name="matmul_res",
    )(a, w, res)


def _gated_conv(b, u, um1, um2, cw):
    y = cw[0:1, :] * um2 + cw[1:2, :] * um1 + cw[2:3, :] * u
    return (b * y).astype(BF16)


def _conv_prompt_kernel(b_ref, u_ref, halo_ref, cw_ref, z_ref, *, tiles_per_seq):
    u = u_ref[...]
    halo = jnp.where(pl.program_id(0) % tiles_per_seq == 0, 0.0, halo_ref[...])
    h1 = halo[SUBLANES - 1:SUBLANES, :]
    h2 = halo[SUBLANES - 2:SUBLANES - 1, :]
    row = lax.broadcasted_iota(jnp.int32, u.shape, 0)
    um1 = jnp.where(row == 0, h1, pltpu.roll(u, 1, axis=0))
    um2 = jnp.where(row == 0, h2, jnp.where(row == 1, h1, pltpu.roll(u, 2, axis=0)))
    z_ref[...] = _gated_conv(b_ref[...], u, um1, um2, cw_ref[...])


def conv_prompt(b, u, conv_w, *, seq, tm):
    m, d = u.shape
    halo_blocks = tm // SUBLANES
    return pl.pallas_call(
        functools.partial(_conv_prompt_kernel, tiles_per_seq=seq // tm),
        out_shape=jax.ShapeDtypeStruct((m, d), BF16),
        grid=(m // tm,),
        in_specs=[
            pl.BlockSpec((tm, d), lambda i: (i, 0)),
            pl.BlockSpec((tm, d), lambda i: (i, 0)),
            pl.BlockSpec((SUBLANES, d), lambda i: (jnp.maximum(i * halo_blocks - 1, 0), 0)),
            pl.BlockSpec((CONV_WIDTH, d), lambda i: (0, 0)),
        ],
        out_specs=pl.BlockSpec((tm, d), lambda i: (i, 0)),
        compiler_params=_params("parallel"),
        name="conv_prompt",
    )(b, u, u, conv_w)


def _conv_sample_kernel(b_ref, u_ref, p1_ref, p2_ref, cw_ref, z_ref, *, t_new):
    u = u_ref[...]
    t = lax.broadcasted_iota(jnp.int32, u.shape, 0) % t_new
    um1 = jnp.where(t >= 1, pltpu.roll(u, 1, axis=0), p1_ref[...])
    um2 = jnp.where(t >= 2, pltpu.roll(u, 2, axis=0), p2_ref[...])
    z_ref[...] = _gated_conv(b_ref[...], u, um1, um2, cw_ref[...])


def conv_sample(b, u, prev1, prev2, conv_w, *, t_new):
    m, d = u.shape
    full = pl.BlockSpec((m, d), lambda i: (0, 0))
    return pl.pallas_call(
        functools.partial(_conv_sample_kernel, t_new=t_new),
        out_shape=jax.ShapeDtypeStruct((m, d), BF16),
        grid=(1,),
        in_specs=[full, full, full, full, pl.BlockSpec((CONV_WIDTH, d), lambda i: (0, 0))],
        out_specs=full,
        compiler_params=_params("arbitrary"),
        name="conv_sample",
    )(b, u, prev1, prev2, conv_w)


def _rows(start, size, stride):
    return pl.ds(start, size) if stride == 1 else pl.ds(start, size, stride=stride)


def _softmax_two_blocks(s_p, s_c):
    m = jnp.maximum(jnp.max(s_p, axis=-1, keepdims=True), jnp.max(s_c, axis=-1, keepdims=True))
    e_p = jnp.exp(s_p - m)
    e_c = jnp.exp(s_c - m)
    den = jnp.sum(e_p, axis=-1, keepdims=True) + jnp.sum(e_c, axis=-1, keepdims=True)
    inv = 1.0 / den
    return (e_p * inv).astype(BF16), (e_c * inv).astype(BF16), m + jnp.log(den)


def _attn_prompt_kernel(q_ref, k_ref, v_ref, kp_ref, vp_ref, o_ref, lse_ref, *, dil, tile):
    n = pl.program_id(1)
    hg = pl.program_id(2)
    blk = ATTN_BLOCK
    scale = HEAD_DIM ** -0.5

    @pl.when(hg == 0)
    def _():
        lse_ref[...] = jnp.zeros_like(lse_ref)

    qi = lax.broadcasted_iota(jnp.int32, (blk, blk), 0)
    ki = lax.broadcasted_iota(jnp.int32, (blk, blk), 1)
    lane = lax.broadcasted_iota(jnp.int32, (blk, LANES), 1)
    cur_mask = ki <= qi
    prev_band = ki >= qi
    for r in range(dil):
        for qb in range(tile // (blk * dil)):
            start = r + dil * blk * qb
            rows = _rows(start, blk, dil)
            q = q_ref[0, rows, :].astype(BF16)
            kc = k_ref[0, rows, :].astype(BF16)
            vc = v_ref[0, rows, :].astype(BF16)
            if qb == 0:
                prows = _rows(r, blk, dil)
                kp = kp_ref[0, prows, :].astype(BF16)
                vp = vp_ref[0, prows, :].astype(BF16)
                prev_mask = jnp.logical_and(prev_band, n > 0)
            else:
                prows = _rows(start - dil * blk, blk, dil)
                kp = k_ref[0, prows, :].astype(BF16)
                vp = v_ref[0, prows, :].astype(BF16)
                prev_mask = prev_band
            s_c = jnp.where(cur_mask, _dot_nt(q, kc) * scale, NEG_INF)
            s_p = jnp.where(prev_mask, _dot_nt(q, kp) * scale, NEG_INF)
            p_p, p_c, lse = _softmax_two_blocks(s_p, s_c)
            o_ref[0, rows, :] = _dot(p_p, vp) + _dot(p_c, vc)
            lse_ref[0, rows, :] = jnp.where(lane == hg, lse, lse_ref[0, rows, :])


def attn_prompt_group(qkv, gi):
    bsz, seq, _ = qkv.shape
    dil = GROUP_DILATIONS[gi]
    hgn = GROUP_HEADS[gi]
    lo = HEAD_OFFSETS[gi]
    tile = ATTN_TILE
    assert seq % tile == 0 and tile % (ATTN_BLOCK * dil) == 0
    assert GROUP_WINDOWS[gi] == ATTN_BLOCK * dil
    prev_rows = ATTN_BLOCK * dil
    ratio = tile // prev_rows

    def prev_map(col0):
        return lambda b, n, h: (b, jnp.maximum(n * ratio - 1, 0), col0 + lo + h)

    return pl.pallas_call(
        functools.partial(_attn_prompt_kernel, dil=dil, tile=tile),
        out_shape=(jax.ShapeDtypeStruct((bsz, seq, hgn * HEAD_DIM), F32),
                   jax.ShapeDtypeStruct((bsz, seq, LANES), F32)),
        grid=(bsz, seq // tile, hgn),
        in_specs=[
            pl.BlockSpec((1, tile, HEAD_DIM), lambda b, n, h: (b, n, lo + h)),
            pl.BlockSpec((1, tile, HEAD_DIM), lambda b, n, h: (b, n, N_HEADS + lo + h)),
            pl.BlockSpec((1, tile, HEAD_DIM), lambda b, n, h: (b, n, 2 * N_HEADS + lo + h)),
            pl.BlockSpec((1, prev_rows, HEAD_DIM), prev_map(N_HEADS)),
            pl.BlockSpec((1, prev_rows, HEAD_DIM), prev_map(2 * N_HEADS)),
        ],
        out_specs=(pl.BlockSpec((1, tile, HEAD_DIM), lambda b, n, h: (b, n, h)),
                   pl.BlockSpec((1, tile, LANES), lambda b, n, h: (b, n, 0))),
        compiler_params=_params("parallel", "parallel", "arbitrary"),
        name=f"attn_prompt_g{gi}",
    )(qkv, qkv, qkv, qkv, qkv)


def _attn_sample_kernel(q_ref, kn_ref, vn_ref, kc_ref, vc_ref, o_ref, lse_ref, *, dil, window, t_new):
    hg = pl.program_id(1)
    past = kc_ref.shape[1]
    rows = q_ref.shape[1]
    scale = HEAD_DIM ** -0.5

    @pl.when(hg == 0)
    def _():
        lse_ref[...] = jnp.zeros_like(lse_ref)

    q = q_ref[0].astype(BF16)
    t_c = lax.broadcasted_iota(jnp.int32, (rows, past), 0)
    p_c = lax.broadcasted_iota(jnp.int32, (rows, past), 1)
    delta = past + t_c - p_c
    mask_c = jnp.logical_and(delta % dil == 0, delta <= window)
    s_c = jnp.where(mask_c, _dot_nt(q, kc_ref[0].astype(BF16)) * scale, NEG_INF)
    t_n = lax.broadcasted_iota(jnp.int32, (rows, rows), 0)
    s_n = lax.broadcasted_iota(jnp.int32, (rows, rows), 1)
    dn = t_n - s_n
    mask_n = jnp.logical_and(jnp.logical_and(dn >= 0, dn % dil == 0),
                             jnp.logical_and(dn <= window, s_n < t_new))
    s_n = jnp.where(mask_n, _dot_nt(q, kn_ref[0].astype(BF16)) * scale, NEG_INF)
    p_c, p_n, lse = _softmax_two_blocks(s_c, s_n)
    o_ref[0] = _dot(p_c, vc_ref[0].astype(BF16)) + _dot(p_n, vn_ref[0].astype(BF16))
    lane = lax.broadcasted_iota(jnp.int32, (rows, LANES), 1)
    lse_ref[0] = jnp.where(lane == hg, lse, lse_ref[0])


def attn_sample_group(qkv_pad, cache, gi, *, t_new):
    dbs, rows, _ = qkv_pad.shape
    past = cache.shape[1]
    hgn = GROUP_HEADS[gi]
    lo = HEAD_OFFSETS[gi]
    return pl.pallas_call(
        functools.partial(_attn_sample_kernel, dil=GROUP_DILATIONS[gi], window=GROUP_WINDOWS[gi],
                          t_new=t_new),
        out_shape=(jax.ShapeDtypeStruct((dbs, rows, hgn * HEAD_DIM), F32),
                   jax.ShapeDtypeStruct((dbs, rows, LANES), F32)),
        grid=(dbs, hgn),
        in_specs=[
            pl.BlockSpec((1, rows, HEAD_DIM), lambda b, h: (b, 0, lo + h)),
            pl.BlockSpec((1, rows, HEAD_DIM), lambda b, h: (b, 0, N_HEADS + lo + h)),
            pl.BlockSpec((1, rows, HEAD_DIM), lambda b, h: (b, 0, 2 * N_HEADS + lo + h)),
            pl.BlockSpec((1, past, HEAD_DIM), lambda b, h: (b, 0, h)),
            pl.BlockSpec((1, past, HEAD_DIM), lambda b, h: (b, 0, hgn + h)),
        ],
        out_specs=(pl.BlockSpec((1, rows, HEAD_DIM), lambda b, h: (b, 0, h)),
                   pl.BlockSpec((1, rows, LANES), lambda b, h: (b, 0, 0))),
        compiler_params=_params("parallel", "arbitrary"),
        name=f"attn_sample_g{gi}",
    )(qkv_pad, qkv_pad, qkv_pad, cache, cache)


def _merge_kernel(o0_ref, o1_ref, o2_ref, l0_ref, l1_ref, l2_ref, m_ref):
    o_refs = (o0_ref, o1_ref, o2_ref)
    l_refs = (l0_ref, l1_ref, l2_ref)
    log_den = []
    for gi in range(N_GROUPS):
        lse = l_refs[gi][...]
        lane = lax.broadcasted_iota(jnp.int32, lse.shape, 1)
        valid = lane < GROUP_HEADS[gi]
        mx = jnp.max(jnp.where(valid, lse, -jnp.inf), axis=-1, keepdims=True)
        ssum = jnp.sum(jnp.where(valid, jnp.exp(lse - mx), 0.0), axis=-1, keepdims=True)
        log_den.append(mx + jnp.log(ssum) - math.log(GROUP_HEADS[gi]))
    top = jnp.maximum(jnp.maximum(log_den[0], log_den[1]), log_den[2])
    ex = [jnp.exp(ld - top) for ld in log_den]
    tot = ex[0] + ex[1] + ex[2]
    for gi in range(N_GROUPS):
        alpha = N_GROUPS * (ex[gi] / tot)
        cols = slice(HEAD_OFFSETS[gi] * HEAD_DIM, HEAD_OFFSETS[gi + 1] * HEAD_DIM)
        m_ref[:, cols] = (o_refs[gi][...] * alpha).astype(BF16)


def merge_groups(outs, lses, *, tm):
    m = outs[0].shape[0]
    in_specs = [pl.BlockSpec((tm, o.shape[1]), lambda i: (i, 0)) for o in outs]
    in_specs += [pl.BlockSpec((tm, LANES), lambda i: (i, 0)) for _ in lses]
    return pl.pallas_call(
        _merge_kernel,
        out_shape=jax.ShapeDtypeStruct((m, N_HEADS * HEAD_DIM), BF16),
        grid=(m // tm,),
        in_specs=in_specs,
        out_specs=pl.BlockSpec((tm, N_HEADS * HEAD_DIM), lambda i: (i, 0)),
        compiler_params=_params("parallel"),
        name="merge_groups",
    )(*outs, *lses)


def _router_kernel(x_ref, g_ref, rw_ref, o_ref):
    h = _rmsnorm_rows(x_ref[...], g_ref[...]).astype(BF16)
    logits = _dot(h, rw_ref[...].astype(BF16))
    lane = lax.broadcasted_iota(jnp.int32, logits.shape, 1)
    valid = lane < N_EXPERTS
    mx = jnp.max(jnp.where(valid, logits, -jnp.inf), axis=-1, keepdims=True)
    ex = jnp.where(valid, jnp.exp(logits - mx), 0.0)
    probs = ex / jnp.sum(ex, axis=-1, keepdims=True)
    probs = jnp.where(valid, probs, -1.0)
    p1 = jnp.max(probs, axis=-1, keepdims=True)
    e1 = jnp.min(jnp.where(probs == p1, lane, LANES), axis=-1, keepdims=True)
    rest = jnp.where(lane == e1, -1.0, probs)
    p2 = jnp.max(rest, axis=-1, keepdims=True)
    e2 = jnp.min(jnp.where(rest == p2, lane, LANES), axis=-1, keepdims=True)
    tot = p1 + p2
    out = jnp.where(lane == 0, p1 / tot, 0.0)
    out = jnp.where(lane == 1, p2 / tot, out)
    out = jnp.where(lane == 2, e1.astype(F32), out)
    out = jnp.where(lane == 3, e2.astype(F32), out)
    o_ref[...] = out


def router(x, gain, router_w_pad, *, tm):
    m, d = x.shape
    return pl.pallas_call(
        _router_kernel,
        out_shape=jax.ShapeDtypeStruct((m, LANES), F32),
        grid=(m // tm,),
        in_specs=[
            pl.BlockSpec((tm, d), lambda i: (i, 0)),
            pl.BlockSpec((1, d), lambda i: (0, 0)),
            pl.BlockSpec((d, LANES), lambda i: (0, 0)),
        ],
        out_specs=pl.BlockSpec((tm, LANES), lambda i: (i, 0)),
        compiler_params=_params("parallel"),
        name="router",
    )(x, gain.reshape(1, d), router_w_pad)


def _moe_kernel(tile_e_ref, tile_rows_ref, tok_ref, xp_hbm, xs_hbm, g_ref, wg_ref, wu_ref, wd_ref,
                o_ref, x_scr, wg_scr, wu_scr, wd_scr, sem, *, tm, sub, n_prompt):
    del tile_e_ref
    i = pl.program_id(0)
    j = pl.program_id(1)
    rows = tile_rows_ref[i]

    def row_copy(r, tok):
        is_prompt = tok < n_prompt

        def start_from(src, idx):
            pltpu.make_async_copy(src.at[pl.ds(idx, 1)], o_ref.at[pl.ds(r, 1)], sem).start()

        @pl.when(is_prompt)
        def _():
            start_from(xp_hbm, tok)

        @pl.when(jnp.logical_not(is_prompt))
        def _():
            start_from(xs_hbm, tok - n_prompt)

    @pl.when(jnp.logical_and(rows == 0, j == 0))
    def _():
        o_ref[...] = jnp.zeros_like(o_ref)

    @pl.when(rows > 0)
    def _():
        @pl.when(j == 0)
        def _():
            def issue(r, c):
                row_copy(r, tok_ref[i * tm + r])
                return c

            lax.fori_loop(0, tm, issue, 0)

            def wait(r, c):
                pltpu.make_async_copy(xp_hbm.at[pl.ds(0, 1)], o_ref.at[pl.ds(0, 1)], sem).wait()
                return c

            lax.fori_loop(0, tm, wait, 0)
            for s in range(tm // sub):
                rs = slice(s * sub, (s + 1) * sub)
                x_scr[rs, :] = _rmsnorm_rows(o_ref[rs, :], g_ref[...]).astype(BF16)
                o_ref[rs, :] = jnp.zeros((sub, o_ref.shape[1]), F32)

        wg_scr[...] = wg_ref[0].astype(BF16)
        wu_scr[...] = wu_ref[0].astype(BF16)
        wd_scr[...] = wd_ref[0].astype(BF16)
        for s in range(tm // sub):
            @pl.when(rows > s * sub)
            def _():
                rs = slice(s * sub, (s + 1) * sub)
                x = x_scr[rs, :]
                g = _dot(x, wg_scr[...])
                u = _dot(x, wu_scr[...])
                a = (g * jax.nn.sigmoid(g) * u).astype(BF16)
                o_ref[rs, :] += _dot(a, wd_scr[...])


def moe_experts(tile_e, tile_rows, tok_of_slot, xp, xs, gain, w_gu, w_dn, *, tm, tf, sub):
    n_tiles = tile_e.shape[0]
    d = xp.shape[1]
    f = w_dn.shape[1]
    nf = f // tf
    last = nf - 1

    def fchunk(i, j, rows):
        return jnp.where(rows[i] > 0, j, last)

    grid_spec = pltpu.PrefetchScalarGridSpec(
        num_scalar_prefetch=3,
        grid=(n_tiles, nf),
        in_specs=[
            pl.BlockSpec(memory_space=pl.ANY),
            pl.BlockSpec(memory_space=pl.ANY),
            pl.BlockSpec((1, d), lambda i, j, te, tr, tk: (0, 0)),
            pl.BlockSpec((1, d, tf), lambda i, j, te, tr, tk: (te[i], 0, fchunk(i, j, tr))),
            pl.BlockSpec((1, d, tf), lambda i, j, te, tr, tk: (te[i], 0, nf + fchunk(i, j, tr))),
            pl.BlockSpec((1, tf, d), lambda i, j, te, tr, tk: (te[i], fchunk(i, j, tr), 0)),
        ],
        out_specs=pl.BlockSpec((tm, d), lambda i, j, te, tr, tk: (i, 0), pipeline_mode=pl.Buffered(1)),
        scratch_shapes=[
            pltpu.VMEM((tm, d), BF16),
            pltpu.VMEM((d, tf), BF16),
            pltpu.VMEM((d, tf), BF16),
            pltpu.VMEM((tf, d), BF16),
            pltpu.SemaphoreType.DMA,
        ],
    )
    return pl.pallas_call(
        functools.partial(_moe_kernel, tm=tm, sub=sub, n_prompt=xp.shape[0]),
        out_shape=jax.ShapeDtypeStruct((n_tiles * tm, d), F32),
        grid_spec=grid_spec,
        compiler_params=_params("arbitrary", "arbitrary"),
        name="moe_experts",
    )(tile_e, tile_rows, tok_of_slot, xp, xs, gain.reshape(1, d), w_gu, w_gu, w_dn)


def _combine_kernel(slot_ref, ys_hbm, x_ref, gate_ref, o_ref, buf, sem, *, tm, tok0):
    i = pl.program_id(0)

    def issue(r, c):
        for k in range(TOP_K):
            slot = slot_ref[(tok0 + i * tm + r) * TOP_K + k]
            pltpu.make_async_copy(ys_hbm.at[pl.ds(slot, 1)], buf.at[k, pl.ds(r, 1)], sem).start()
        return c

    lax.fori_loop(0, tm, issue, 0)

    def wait(r, c):
        pltpu.make_async_copy(ys_hbm.at[pl.ds(0, 1)], buf.at[0, pl.ds(0, 1)], sem).wait()
        return c

    lax.fori_loop(0, tm * TOP_K, wait, 0)
    gates = gate_ref[...]
    o_ref[...] = x_ref[...] + gates[:, 0:1] * buf[0] + gates[:, 1:2] * buf[1]


def moe_combine(slots_flat, ys, x, route, *, tm, tok0):
    m, d = x.shape
    grid_spec = pltpu.PrefetchScalarGridSpec(
        num_scalar_prefetch=1,
        grid=(m // tm,),
        in_specs=[
            pl.BlockSpec(memory_space=pl.ANY),
            pl.BlockSpec((tm, d), lambda i, s: (i, 0)),
            pl.BlockSpec((tm, LANES), lambda i, s: (i, 0)),
        ],
        out_specs=pl.BlockSpec((tm, d), lambda i, s: (i, 0)),
        scratch_shapes=[pltpu.VMEM((TOP_K, tm, d), F32), pltpu.SemaphoreType.DMA],
    )
    return pl.pallas_call(
        functools.partial(_combine_kernel, tm=tm, tok0=tok0),
        out_shape=jax.ShapeDtypeStruct((m, d), F32),
        grid_spec=grid_spec,
        compiler_params=_params("arbitrary"),
        name="moe_combine",
    )(slots_flat, ys, x, route)


def _routing_tables(route, *, tm, n_tiles):
    n_tok = route.shape[0]
    expert = route[:, 2:4].astype(jnp.int32).reshape(-1)
    n_assign = expert.shape[0]
    order = jnp.argsort(expert, stable=True).astype(jnp.int32)
    sorted_e = expert[order]
    counts = jnp.sum(expert[:, None] == jnp.arange(N_EXPERTS)[None, :], axis=0).astype(jnp.int32)
    tiles_e = (counts + tm - 1) // tm
    tile_end = jnp.cumsum(tiles_e)
    tile_start = tile_end - tiles_e
    sorted_off = jnp.cumsum(counts) - counts
    slot_sorted = tile_start[sorted_e] * tm + (jnp.arange(n_assign, dtype=jnp.int32) - sorted_off[sorted_e])
    tok_of_slot = jnp.zeros((n_tiles * tm,), jnp.int32).at[slot_sorted].set(order // TOP_K)
    slot_of_assign = jnp.zeros((n_assign,), jnp.int32).at[order].set(slot_sorted)
    tiles = jnp.arange(n_tiles, dtype=jnp.int32)
    used = tile_end[-1]
    te = jnp.minimum(jnp.searchsorted(tile_end, tiles, side="right"), N_EXPERTS - 1).astype(jnp.int32)
    rows = jnp.clip(counts[te] - (tiles - tile_start[te]) * tm, 0, tm)
    rows = jnp.where(tiles < used, rows, 0).astype(jnp.int32)
    last_e = te[jnp.maximum(used - 1, 0)]
    te = jnp.where(tiles < used, te, last_e).astype(jnp.int32)
    del n_tok
    return te, rows, tok_of_slot, slot_of_assign


def _row_tile(m, target):
    return target if m % target == 0 else m


def kernel(x_prompt, x_sample, state_conv, cache_kv_w128, cache_kv_w512, cache_kv_w2048, norm_mix_w, norm_ffn_w, conv_in_w, conv_w, conv_out_w, attn_qkv_w, q_norm_w, k_norm_w, attn_out_w, ffn_gate_up_w, ffn_down_w, router_w, moe_gate_up_w, moe_down_w):
    bsz, seq, d = x_prompt.shape
    dbs, t_new, _ = x_sample.shape
    mp = bsz * seq
    ms = dbs * t_new
    xp = x_prompt.reshape(mp, d)
    xs = x_sample.reshape(ms, d)
    caches = (cache_kv_w128, cache_kv_w512, cache_kv_w2048)
    for gi in range(N_GROUPS):
        assert caches[gi].shape[2] == GROUP_WINDOWS[gi]
    assert t_new >= CONV_WIDTH - 1 and t_new <= SUBLANES

    tmp = _row_tile(mp, 1024)

    b_p, u_p = conv_in_proj(xp, norm_mix_w[0], conv_in_w[0], tm=tmp, tn=256)
    b_s, u_s = conv_in_proj(xs, norm_mix_w[0], conv_in_w[0], tm=ms, tn=512)
    z_p = conv_prompt(b_p, u_p, conv_w[0], seq=seq, tm=256)
    st = state_conv[0]
    prev1 = jnp.pad(st[:, 1:2], ((0, 0), (0, t_new - 1), (0, 0))).reshape(ms, d)
    prev2 = jnp.pad(st, ((0, 0), (0, t_new - 2), (0, 0))).reshape(ms, d)
    z_s = conv_sample(b_s, u_s, prev1, prev2, conv_w[0], t_new=t_new)
    xp = matmul_res(z_p, conv_out_w[0], xp, tm=tmp, tn=512)
    xs = matmul_res(z_s, conv_out_w[0], xs, tm=ms, tn=512)
    new_conv_prompt = u_p.reshape(bsz, seq, d)[:, -(CONV_WIDTH - 1):][None]
    new_conv_sample = u_s.reshape(dbs, t_new, d)[:, -(CONV_WIDTH - 1):][None]

    a_p = ffn_up(xp, norm_ffn_w[0], ffn_gate_up_w[0], tm=tmp, tn=512)
    a_s = ffn_up(xs, norm_ffn_w[0], ffn_gate_up_w[0], tm=ms, tn=512)
    xp = matmul_res(a_p, ffn_down_w[0], xp, tm=tmp, tn=256)
    xs = matmul_res(a_s, ffn_down_w[0], xs, tm=ms, tn=256)

    qkv_p = qkv_proj(xp, norm_mix_w[1], attn_qkv_w[0], q_norm_w[0], k_norm_w[0], tm=tmp, tn=512)
    qkv_s = qkv_proj(xs, norm_mix_w[1], attn_qkv_w[0], q_norm_w[0], k_norm_w[0], tm=ms, tn=512)
    qkv_p3 = qkv_p.reshape(bsz, seq, 3 * d)
    qkv_s3 = qkv_s.reshape(dbs, t_new, 3 * d)
    qkv_s_pad = jnp.pad(qkv_s3, ((0, 0), (0, SUBLANES - t_new), (0, 0)))

    outs_p, lses_p, outs_s, lses_s = [], [], [], []
    kv_prompt, kv_sample = [], []
    for gi in range(N_GROUPS):
        lo, hi = HEAD_OFFSETS[gi], HEAD_OFFSETS[gi + 1]
        hgn = hi - lo
        o, lse = attn_prompt_group(qkv_p3, gi)
        outs_p.append(o.reshape(mp, hgn * HEAD_DIM))
        lses_p.append(lse.reshape(mp, LANES))
        cache = caches[gi][0]
        past = cache.shape[1]
        o, lse = attn_sample_group(qkv_s_pad, cache.reshape(dbs, past, 2 * hgn * HEAD_DIM), gi, t_new=t_new)
        outs_s.append(o[:, :t_new].reshape(ms, hgn * HEAD_DIM))
        lses_s.append(lse[:, :t_new].reshape(ms, LANES))
        keep = min(GROUP_WINDOWS[gi], seq)
        kv_tail = qkv_p3[:, seq - keep:, d:].reshape(bsz, keep, 2, N_HEADS, HEAD_DIM)[:, :, :, lo:hi]
        kv_prompt.append(kv_tail[None])
        kv_new = qkv_s3[:, :, d:].reshape(dbs, t_new, 2, N_HEADS, HEAD_DIM)[:, :, :, lo:hi]
        kv_sample.append(jnp.concatenate([cache, kv_new], axis=1)[:, t_new:][None])

    merged_p = merge_groups(outs_p, lses_p, tm=_row_tile(mp, 512))
    merged_s = merge_groups(outs_s, lses_s, tm=ms)
    xp = matmul_res(merged_p, attn_out_w[0], xp, tm=tmp, tn=512)
    xs = matmul_res(merged_s, attn_out_w[0], xs, tm=ms, tn=512)

    rw_pad = jnp.pad(router_w[0], ((0, 0), (0, LANES - N_EXPERTS)))
    route_p = router(xp, norm_ffn_w[1], rw_pad, tm=_row_tile(mp, 512))
    route_s = router(xs, norm_ffn_w[1], rw_pad, tm=ms)
    route = jnp.concatenate([route_p, route_s], axis=0)
    tm_moe = 1024
    n_tiles = ((mp + ms) * TOP_K) // tm_moe + N_EXPERTS
    tile_e, tile_rows, tok_of_slot, slot_of_assign = _routing_tables(route, tm=tm_moe, n_tiles=n_tiles)
    ys = moe_experts(tile_e, tile_rows, tok_of_slot, xp, xs, norm_ffn_w[1], moe_gate_up_w[0], moe_down_w[0],
                     tm=tm_moe, tf=512, sub=256)
    y_p = moe_combine(slot_of_assign, ys, xp, route_p, tm=_row_tile(mp, 256), tok0=0)
    y_s = moe_combine(slot_of_assign, ys, xs, route_s, tm=ms, tok0=mp)

    return (y_p.reshape(bsz, seq, d), y_s.reshape(dbs, t_new, d), new_conv_prompt,
            kv_prompt[0], kv_prompt[1], kv_prompt[2], new_conv_sample,
            kv_sample[0], kv_sample[1], kv_sample[2])
```

```python
import functools
import math

import jax
import jax.numpy as jnp
from jax import lax
from jax.experimental import pallas as pl
from jax.experimental.pallas import tpu as pltpu

D_MODEL = 2048
N_HEADS = 16
HEAD_DIM = 128
GROUP_WINDOWS = (128, 512, 2048)
GROUP_DILATIONS = (1, 4, 16)
GROUP_HEADS = (6, 5, 5)
HEAD_OFFSETS = (0, 6, 11, 16)
N_GROUPS = 3
CONV_WIDTH = 3
N_EXPERTS = 8
TOP_K = 2
RMS_EPS = 1e-6
NEG_INF = -1e30

LANES = 128
SUBLANES = 8
ATTN_BLOCK = 128
ATTN_TILE = 2048
VMEM_LIMIT = 56 * 1024 * 1024
DMA_UNROLL = 8

BF16 = jnp.bfloat16
F32 = jnp.float32


def _params(*sem):
    return pltpu.CompilerParams(dimension_semantics=sem, vmem_limit_bytes=VMEM_LIMIT)


def _dot(a, b, mxu=BF16):
    if mxu == F32:
        return jnp.dot(a, b, preferred_element_type=F32, precision=lax.Precision.HIGHEST)
    return jnp.dot(a.astype(BF16), b.astype(BF16), preferred_element_type=F32)


def _dot_nt(a, b, mxu=BF16):
    dims = (((1,), (1,)), ((), ()))
    if mxu == F32:
        return lax.dot_general(a, b, dims, preferred_element_type=F32, precision=lax.Precision.HIGHEST)
    return lax.dot_general(a.astype(BF16), b.astype(BF16), dims, preferred_element_type=F32)


def _rmsnorm_rows(x, gain):
    ms = jnp.mean(x * x, axis=-1, keepdims=True)
    return (x * lax.rsqrt(ms + RMS_EPS)) * gain


def _conv_in_kernel(x_ref, g_ref, wb_ref, wc_ref, wh_ref, b_ref, u_ref, h_scr):
    mxu = h_scr.dtype

    @pl.when(pl.program_id(1) == 0)
    def _():
        h_scr[...] = _rmsnorm_rows(x_ref[...], g_ref[...]).astype(mxu)

    h = h_scr[...]
    b_ref[...] = _dot(h, wb_ref[...], mxu)
    c = _dot(h, wc_ref[...], mxu)
    hx = _dot(h, wh_ref[...], mxu)
    u_ref[...] = c * hx


def conv_in_proj(x, gain, w, *, tm, tn, mxu=BF16):
    m, d = x.shape
    nj = d // tn
    return pl.pallas_call(
        _conv_in_kernel,
        out_shape=(jax.ShapeDtypeStruct((m, d), F32), jax.ShapeDtypeStruct((m, d), F32)),
        grid=(m // tm, nj),
        in_specs=[
            pl.BlockSpec((tm, d), lambda i, j: (i, 0)),
            pl.BlockSpec((1, d), lambda i, j: (0, 0)),
            pl.BlockSpec((d, tn), lambda i, j: (0, j)),
            pl.BlockSpec((d, tn), lambda i, j: (0, j + nj)),
            pl.BlockSpec((d, tn), lambda i, j: (0, j + 2 * nj)),
        ],
        out_specs=(pl.BlockSpec((tm, tn), lambda i, j: (i, j)),
                   pl.BlockSpec((tm, tn), lambda i, j: (i, j))),
        scratch_shapes=[pltpu.VMEM((tm, d), mxu)],
        compiler_params=_params("parallel", "arbitrary"),
        name="conv_in_proj",
    )(x, gain.reshape(1, d), w, w, w)


def _ffn_up_kernel(x_ref, g_ref, wg_ref, wu_ref, a_ref, h_scr):
    mxu = h_scr.dtype

    @pl.when(pl.program_id(1) == 0)
    def _():
        h_scr[...] = _rmsnorm_rows(x_ref[...], g_ref[...]).astype(mxu)

    h = h_scr[...]
    g = _dot(h, wg_ref[...], mxu)
    u = _dot(h, wu_ref[...], mxu)
    a_ref[...] = (g * jax.nn.sigmoid(g) * u).astype(mxu)


def ffn_up(x, gain, w, *, tm, tn, mxu=BF16):
    m, d = x.shape
    f = w.shape[1] // 2
    nj = f // tn
    return pl.pallas_call(
        _ffn_up_kernel,
        out_shape=jax.ShapeDtypeStruct((m, f), mxu),
        grid=(m // tm, nj),
        in_specs=[
            pl.BlockSpec((tm, d), lambda i, j: (i, 0)),
            pl.BlockSpec((1, d), lambda i, j: (0, 0)),
            pl.BlockSpec((d, tn), lambda i, j: (0, j)),
            pl.BlockSpec((d, tn), lambda i, j: (0, j + nj)),
        ],
        out_specs=pl.BlockSpec((tm, tn), lambda i, j: (i, j)),
        scratch_shapes=[pltpu.VMEM((tm, d), mxu)],
        compiler_params=_params("parallel", "arbitrary"),
        name="ffn_up",
    )(x, gain.reshape(1, d), w, w)


def _qkv_kernel(x_ref, g_ref, w_ref, qn_ref, kn_ref, o_ref, h_scr, *, tn, d_model):
    j = pl.program_id(1)
    mxu = h_scr.dtype

    @pl.when(j == 0)
    def _():
        h_scr[...] = _rmsnorm_rows(x_ref[...], g_ref[...]).astype(mxu)

    o_ref[...] = _dot(h_scr[...], w_ref[...], mxu)
    nq = d_model // tn

    @pl.when(j < 2 * nq)
    def _():
        gain = jnp.where(j < nq, qn_ref[...], kn_ref[...])
        for hh in range(tn // HEAD_DIM):
            cols = slice(hh * HEAD_DIM, (hh + 1) * HEAD_DIM)
            o_ref[:, cols] = _rmsnorm_rows(o_ref[:, cols], gain)


def qkv_proj(x, gain, w, q_norm_w, k_norm_w, *, tm, tn, mxu=BF16):
    m, d = x.shape
    n = w.shape[1]
    return pl.pallas_call(
        functools.partial(_qkv_kernel, tn=tn, d_model=d),
        out_shape=jax.ShapeDtypeStruct((m, n), F32),
        grid=(m // tm, n // tn),
        in_specs=[
            pl.BlockSpec((tm, d), lambda i, j: (i, 0)),
            pl.BlockSpec((1, d), lambda i, j: (0, 0)),
            pl.BlockSpec((d, tn), lambda i, j: (0, j)),
            pl.BlockSpec((1, HEAD_DIM), lambda i, j: (0, 0)),
            pl.BlockSpec((1, HEAD_DIM), lambda i, j: (0, 0)),
        ],
        out_specs=pl.BlockSpec((tm, tn), lambda i, j: (i, j)),
        scratch_shapes=[pltpu.VMEM((tm, d), mxu)],
        compiler_params=_params("parallel", "arbitrary"),
        name="qkv_proj",
    )(x, gain.reshape(1, d), w, q_norm_w.reshape(1, HEAD_DIM), k_norm_w.reshape(1, HEAD_DIM))


def _matmul_res_kernel(a_ref, w_ref, r_ref, o_ref):
    o_ref[...] = r_ref[...] + _dot(a_ref[...], w_ref[...], a_ref.dtype)


def matmul_res(a, w, res, *, tm, tn):
    m, k = a.shape
    n = w.shape[1]
    return pl.pallas_call(
        _matmul_res_kernel,
        out_shape=jax.ShapeDtypeStruct((m, n), F32),
        grid=(m // tm, n // tn),
        in_specs=[
            pl.BlockSpec((tm, k), lambda i, j: (i, 0)),
            pl.BlockSpec((k, tn), lambda i, j: (0, j)),
            pl.BlockSpec((tm, tn), lambda i, j: (i, j)),
        ],
        out_specs=pl.BlockSpec((tm, tn), lambda i, j: (i, j)),
        compiler_params=_params("parallel", "arbitrary"),
        name="matmul_res",
    )(a, w, res)


def _matmul_res_joint_kernel(ap_ref, as_ref, w_ref, rp_ref, rs_ref, o_ref, *, n_prompt_tiles):
    i = pl.program_id(0)

    @pl.when(i < n_prompt_tiles)
    def _():
        o_ref[...] = rp_ref[...] + _dot(ap_ref[...], w_ref[...], ap_ref.dtype)

    @pl.when(i == n_prompt_tiles)
    def _():
        o_ref[0:as_ref.shape[0], :] = rs_ref[...] + _dot(as_ref[...], w_ref[...], as_ref.dtype)


def matmul_res_joint(a_p, a_s, w, res_p, res_s, *, tm, tn):
    mp, k = a_p.shape
    ms = a_s.shape[0]
    n = w.shape[1]
    nt = mp // tm
    return pl.pallas_call(
        functools.partial(_matmul_res_joint_kernel, n_prompt_tiles=nt),
        out_shape=jax.ShapeDtypeStruct((mp + ms, n), F32),
        grid=(nt + 1, n // tn),
        in_specs=[
            pl.BlockSpec((tm, k), lambda i, j: (jnp.minimum(i, nt - 1), 0)),
            pl.BlockSpec((ms, k), lambda i, j: (0, 0)),
            pl.BlockSpec((k, tn), lambda i, j: (0, j)),
            pl.BlockSpec((tm, tn), lambda i, j: (jnp.minimum(i, nt - 1), j)),
            pl.BlockSpec((ms, tn), lambda i, j: (0, j)),
        ],
        out_specs=pl.BlockSpec((tm, tn), lambda i, j: (i, j)),
        compiler_params=_params("arbitrary", "arbitrary"),
        name="matmul_res_joint",
    )(a_p, a_s, w, res_p, res_s)


def _gated_conv(b, u, um1, um2, cw):
    return b * (cw[0:1, :] * um2 + cw[1:2, :] * um1 + cw[2:3, :] * u)


def _conv_prompt_kernel(b_ref, u_ref, halo_ref, cw_ref, z_ref, *, tiles_per_seq):
    u = u_ref[...]
    halo = jnp.where(pl.program_id(0) % tiles_per_seq == 0, 0.0, halo_ref[...])
    h1 = halo[SUBLANES - 1:SUBLANES, :]
    h2 = halo[SUBLANES - 2:SUBLANES - 1, :]
    row = lax.broadcasted_iota(jnp.int32, u.shape, 0)
    um1 = jnp.where(row == 0, h1, pltpu.roll(u, 1, axis=0))
    um2 = jnp.where(row == 0, h2, jnp.where(row == 1, h1, pltpu.roll(u, 2, axis=0)))
    z_ref[...] = _gated_conv(b_ref[...], u, um1, um2, cw_ref[...]).astype(z_ref.dtype)


def conv_prompt(b, u, conv_w, *, seq, tm):
    m, d = u.shape
    halo_blocks = tm // SUBLANES
    return pl.pallas_call(
        functools.partial(_conv_prompt_kernel, tiles_per_seq=seq // tm),
        out_shape=jax.ShapeDtypeStruct((m, d), BF16),
        grid=(m // tm,),
        in_specs=[
            pl.BlockSpec((tm, d), lambda i: (i, 0)),
            pl.BlockSpec((tm, d), lambda i: (i, 0)),
            pl.BlockSpec((SUBLANES, d), lambda i: (jnp.maximum(i * halo_blocks - 1, 0), 0)),
            pl.BlockSpec((CONV_WIDTH, d), lambda i: (0, 0)),
        ],
        out_specs=pl.BlockSpec((tm, d), lambda i: (i, 0)),
        compiler_params=_params("parallel"),
        name="conv_prompt",
    )(b, u, u, conv_w)


def _conv_sample_kernel(b_ref, u_ref, p1_ref, p2_ref, cw_ref, z_ref, *, t_new):
    u = u_ref[...]
    t = lax.broadcasted_iota(jnp.int32, u.shape, 0) % t_new
    um1 = jnp.where(t >= 1, pltpu.roll(u, 1, axis=0), p1_ref[...])
    um2 = jnp.where(t >= 2, pltpu.roll(u, 2, axis=0), p2_ref[...])
    z_ref[...] = _gated_conv(b_ref[...], u, um1, um2, cw_ref[...]).astype(z_ref.dtype)


def conv_sample(b, u, prev1, prev2, conv_w, *, t_new, out_dtype):
    m, d = u.shape
    full = pl.BlockSpec((m, d), lambda i: (0, 0))
    return pl.pallas_call(
        functools.partial(_conv_sample_kernel, t_new=t_new),
        out_shape=jax.ShapeDtypeStruct((m, d), out_dtype),
        grid=(1,),
        in_specs=[full, full, full, full, pl.BlockSpec((CONV_WIDTH, d), lambda i: (0, 0))],
        out_specs=full,
        compiler_params=_params("arbitrary"),
        name="conv_sample",
    )(b, u, prev1, prev2, conv_w)


def _rows(start, size, stride):
    return pl.ds(start, size) if stride == 1 else pl.ds(start, size, stride=stride)


def _softmax_two_blocks(s_p, s_c, dtype=BF16):
    m = jnp.maximum(jnp.max(s_p, axis=-1, keepdims=True), jnp.max(s_c, axis=-1, keepdims=True))
    e_p = jnp.exp(s_p - m)
    e_c = jnp.exp(s_c - m)
    den = jnp.sum(e_p, axis=-1, keepdims=True) + jnp.sum(e_c, axis=-1, keepdims=True)
    inv = 1.0 / den
    return (e_p * inv).astype(dtype), (e_c * inv).astype(dtype), m + jnp.log(den)


def _attn_prompt_kernel(q_ref, k_ref, v_ref, kp_ref, vp_ref, o_ref, lse_ref, *, dil, tile):
    n = pl.program_id(1)
    hg = pl.program_id(2)
    blk = ATTN_BLOCK
    scale = HEAD_DIM ** -0.5

    @pl.when(hg == 0)
    def _():
        lse_ref[...] = jnp.zeros_like(lse_ref)

    qi = lax.broadcasted_iota(jnp.int32, (blk, blk), 0)
    ki = lax.broadcasted_iota(jnp.int32, (blk, blk), 1)
    lane = lax.broadcasted_iota(jnp.int32, (blk, LANES), 1)
    cur_mask = ki <= qi
    prev_band = ki >= qi
    for r in range(dil):
        for qb in range(tile // (blk * dil)):
            start = r + dil * blk * qb
            rows = _rows(start, blk, dil)
            q = q_ref[0, rows, :].astype(BF16)
            kc = k_ref[0, rows, :].astype(BF16)
            vc = v_ref[0, rows, :].astype(BF16)
            if qb == 0:
                prows = _rows(r, blk, dil)
                kp = kp_ref[0, prows, :].astype(BF16)
                vp = vp_ref[0, prows, :].astype(BF16)
                prev_mask = jnp.logical_and(prev_band, n > 0)
            else:
                prows = _rows(start - dil * blk, blk, dil)
                kp = k_ref[0, prows, :].astype(BF16)
                vp = v_ref[0, prows, :].astype(BF16)
                prev_mask = prev_band
            s_c = jnp.where(cur_mask, _dot_nt(q, kc) * scale, NEG_INF)
            s_p = jnp.where(prev_mask, _dot_nt(q, kp) * scale, NEG_INF)
            p_p, p_c, lse = _softmax_two_blocks(s_p, s_c)
            o_ref[0, rows, :] = _dot(p_p, vp) + _dot(p_c, vc)
            lse_ref[0, rows, :] = jnp.where(lane == hg, lse, lse_ref[0, rows, :])


def attn_prompt_group(qkv, gi):
    bsz, seq, _ = qkv.shape
    dil = GROUP_DILATIONS[gi]
    hgn = GROUP_HEADS[gi]
    lo = HEAD_OFFSETS[gi]
    tile = ATTN_TILE
    assert seq % tile == 0 and tile % (ATTN_BLOCK * dil) == 0
    assert GROUP_WINDOWS[gi] == ATTN_BLOCK * dil
    prev_rows = ATTN_BLOCK * dil
    ratio = tile // prev_rows

    def prev_map(col0):
        return lambda b, n, h: (b, jnp.maximum(n * ratio - 1, 0), col0 + lo + h)

    return pl.pallas_call(
        functools.partial(_attn_prompt_kernel, dil=dil, tile=tile),
        out_shape=(jax.ShapeDtypeStruct((bsz, seq, hgn * HEAD_DIM), F32),
                   jax.ShapeDtypeStruct((bsz, seq, LANES), F32)),
        grid=(bsz, seq // tile, hgn),
        in_specs=[
            pl.BlockSpec((1, tile, HEAD_DIM), lambda b, n, h: (b, n, lo + h)),
            pl.BlockSpec((1, tile, HEAD_DIM), lambda b, n, h: (b, n, N_HEADS + lo + h)),
            pl.BlockSpec((1, tile, HEAD_DIM), lambda b, n, h: (b, n, 2 * N_HEADS + lo + h)),
            pl.BlockSpec((1, prev_rows, HEAD_DIM), prev_map(N_HEADS)),
            pl.BlockSpec((1, prev_rows, HEAD_DIM), prev_map(2 * N_HEADS)),
        ],
        out_specs=(pl.BlockSpec((1, tile, HEAD_DIM), lambda b, n, h: (b, n, h)),
                   pl.BlockSpec((1, tile, LANES), lambda b, n, h: (b, n, 0))),
        compiler_params=_params("parallel", "parallel", "arbitrary"),
        name=f"attn_prompt_g{gi}",
    )(qkv, qkv, qkv, qkv, qkv)


def _kv_pack_kernel(*refs, hgn):
    o_ref = refs[-1]
    tp = refs[0].shape[1]
    for h in range(hgn):
        for kv in range(2):
            o_ref[0, pl.ds(2 * h + kv, tp, stride=2 * hgn), :] = refs[kv * hgn + h][0]


def kv_state_prompt(qkv, gi, *, keep):
    bsz, seq, _ = qkv.shape
    hgn = GROUP_HEADS[gi]
    lo = HEAD_OFFSETS[gi]
    tp = min(keep, 512)
    blk0 = (seq - keep) // tp
    specs = []
    for kv in range(2):
        for h in range(hgn):
            col = (1 + kv) * N_HEADS + lo + h
            specs.append(pl.BlockSpec((1, tp, HEAD_DIM), lambda b, p, col=col: (b, blk0 + p, col)))
    return pl.pallas_call(
        functools.partial(_kv_pack_kernel, hgn=hgn),
        out_shape=jax.ShapeDtypeStruct((bsz, keep * 2 * hgn, HEAD_DIM), F32),
        grid=(bsz, keep // tp),
        in_specs=specs,
        out_specs=pl.BlockSpec((1, tp * 2 * hgn, HEAD_DIM), lambda b, p: (b, p, 0)),
        compiler_params=_params("parallel", "parallel"),
        name=f"kv_state_prompt_g{gi}",
    )(*([qkv] * (2 * hgn)))


def _attn_sample_kernel(qkv_ref, c_ref, o_ref, lse_ref, st_ref, *, gi, t_new):
    dil = GROUP_DILATIONS[gi]
    window = GROUP_WINDOWS[gi]
    hgn = GROUP_HEADS[gi]
    lo = HEAD_OFFSETS[gi]
    per_pos = 2 * hgn
    rows = qkv_ref.shape[1]
    past = c_ref.shape[1] // per_pos
    scale = HEAD_DIM ** -0.5

    shift = t_new * per_pos
    kept = (past - t_new) * per_pos
    st_ref[0, 0:kept, :] = c_ref[0, shift:shift + kept, :]

    t_c = lax.broadcasted_iota(jnp.int32, (rows, past), 0)
    p_c = lax.broadcasted_iota(jnp.int32, (rows, past), 1)
    delta = past + t_c - p_c
    mask_c = jnp.logical_and(delta % dil == 0, delta <= window)
    t_n = lax.broadcasted_iota(jnp.int32, (rows, rows), 0)
    s_n = lax.broadcasted_iota(jnp.int32, (rows, rows), 1)
    dn = t_n - s_n
    mask_n = jnp.logical_and(jnp.logical_and(dn >= 0, dn % dil == 0),
                             jnp.logical_and(dn <= window, s_n < t_new))
    lane = lax.broadcasted_iota(jnp.int32, (rows, LANES), 1)
    lse_all = jnp.zeros((rows, LANES), F32)
    for h in range(hgn):
        def col(part):
            return slice((part * N_HEADS + lo + h) * HEAD_DIM, (part * N_HEADS + lo + h + 1) * HEAD_DIM)

        q = qkv_ref[0, :, col(0)]
        kn = qkv_ref[0, :, col(1)]
        vn = qkv_ref[0, :, col(2)]
        for t in range(t_new):
            row = kept + t * per_pos + 2 * h
            st_ref[0, row:row + 1, :] = kn[t:t + 1]
            st_ref[0, row + 1:row + 2, :] = vn[t:t + 1]
        kc = c_ref[0, pl.ds(2 * h, past, stride=per_pos), :]
        vc = c_ref[0, pl.ds(2 * h + 1, past, stride=per_pos), :]
        s_c = jnp.where(mask_c, _dot_nt(q, kc, F32) * scale, NEG_INF)
        s_new = jnp.where(mask_n, _dot_nt(q, kn, F32) * scale, NEG_INF)
        p_c, p_n, lse = _softmax_two_blocks(s_c, s_new, F32)
        o_ref[0, :, h * HEAD_DIM:(h + 1) * HEAD_DIM] = _dot(p_c, vc, F32) + _dot(p_n, vn, F32)
        lse_all = jnp.where(lane == h, lse, lse_all)
    lse_ref[0] = lse_all


def attn_sample_group(qkv_pad, cache, gi, *, t_new):
    dbs, rows, width = qkv_pad.shape
    crow = cache.shape[1]
    hgn = GROUP_HEADS[gi]
    assert (t_new * 2 * hgn) % SUBLANES == 0
    return pl.pallas_call(
        functools.partial(_attn_sample_kernel, gi=gi, t_new=t_new),
        out_shape=(jax.ShapeDtypeStruct((dbs, rows, hgn * HEAD_DIM), F32),
                   jax.ShapeDtypeStruct((dbs, rows, LANES), F32),
                   jax.ShapeDtypeStruct(cache.shape, F32)),
        grid=(dbs,),
        in_specs=[
            pl.BlockSpec((1, rows, width), lambda b: (b, 0, 0)),
            pl.BlockSpec((1, crow, HEAD_DIM), lambda b: (b, 0, 0)),
        ],
        out_specs=(pl.BlockSpec((1, rows, hgn * HEAD_DIM), lambda b: (b, 0, 0)),
                   pl.BlockSpec((1, rows, LANES), lambda b: (b, 0, 0)),
                   pl.BlockSpec((1, crow, HEAD_DIM), lambda b: (b, 0, 0))),
        compiler_params=_params("parallel"),
        name=f"attn_sample_g{gi}",
    )(qkv_pad, cache)


def _merge_kernel(o0_ref, o1_ref, o2_ref, l0_ref, l1_ref, l2_ref, m_ref):
    o_refs = (o0_ref, o1_ref, o2_ref)
    l_refs = (l0_ref, l1_ref, l2_ref)
    log_den = []
    for gi in range(N_GROUPS):
        lse = l_refs[gi][...]
        lane = lax.broadcasted_iota(jnp.int32, lse.shape, 1)
        valid = lane < GROUP_HEADS[gi]
        mx = jnp.max(jnp.where(valid, lse, -jnp.inf), axis=-1, keepdims=True)
        ssum = jnp.sum(jnp.where(valid, jnp.exp(lse - mx), 0.0), axis=-1, keepdims=True)
        log_den.append(mx + jnp.log(ssum) - math.log(GROUP_HEADS[gi]))
    top = jnp.maximum(jnp.maximum(log_den[0], log_den[1]), log_den[2])
    ex = [jnp.exp(ld - top) for ld in log_den]
    tot = ex[0] + ex[1] + ex[2]
    for gi in range(N_GROUPS):
        alpha = N_GROUPS * (ex[gi] / tot)
        cols = slice(HEAD_OFFSETS[gi] * HEAD_DIM, HEAD_OFFSETS[gi + 1] * HEAD_DIM)
        m_ref[:, cols] = (o_refs[gi][...] * alpha).astype(m_ref.dtype)


def merge_groups(outs, lses, *, tm, out_dtype=BF16):
    m = outs[0].shape[0]
    in_specs = [pl.BlockSpec((tm, o.shape[1]), lambda i: (i, 0)) for o in outs]
    in_specs += [pl.BlockSpec((tm, LANES), lambda i: (i, 0)) for _ in lses]
    return pl.pallas_call(
        _merge_kernel,
        out_shape=jax.ShapeDtypeStruct((m, N_HEADS * HEAD_DIM), out_dtype),
        grid=(m // tm,),
        in_specs=in_specs,
        out_specs=pl.BlockSpec((tm, N_HEADS * HEAD_DIM), lambda i: (i, 0)),
        compiler_params=_params("parallel"),
        name="merge_groups",
    )(*outs, *lses)


def _router_kernel(x_ref, g_ref, rw_ref, o_ref, *, mxu):
    logits = _dot(_rmsnorm_rows(x_ref[...], g_ref[...]), rw_ref[...], mxu)
    lane = lax.broadcasted_iota(jnp.int32, logits.shape, 1)
    valid = lane < N_EXPERTS
    mx = jnp.max(jnp.where(valid, logits, -jnp.inf), axis=-1, keepdims=True)
    ex = jnp.where(valid, jnp.exp(logits - mx), 0.0)
    probs = ex / jnp.sum(ex, axis=-1, keepdims=True)
    probs = jnp.where(valid, probs, -1.0)
    p1 = jnp.max(probs, axis=-1, keepdims=True)
    e1 = jnp.min(jnp.where(probs == p1, lane, LANES), axis=-1, keepdims=True)
    rest = jnp.where(lane == e1, -1.0, probs)
    p2 = jnp.max(rest, axis=-1, keepdims=True)
    e2 = jnp.min(jnp.where(rest == p2, lane, LANES), axis=-1, keepdims=True)
    tot = p1 + p2
    out = jnp.where(lane == 0, p1 / tot, 0.0)
    out = jnp.where(lane == 1, p2 / tot, out)
    out = jnp.where(lane == 2, e1.astype(F32), out)
    out = jnp.where(lane == 3, e2.astype(F32), out)
    o_ref[...] = out


def router(x, gain, router_w_pad, *, rows, tm, row_block0=0, mxu=BF16):
    d = x.shape[1]
    return pl.pallas_call(
        functools.partial(_router_kernel, mxu=mxu),
        out_shape=jax.ShapeDtypeStruct((rows, LANES), F32),
        grid=(rows // tm,),
        in_specs=[
            pl.BlockSpec((tm, d), lambda i: (row_block0 + i, 0)),
            pl.BlockSpec((1, d), lambda i: (0, 0)),
            pl.BlockSpec((d, LANES), lambda i: (0, 0)),
        ],
        out_specs=pl.BlockSpec((tm, LANES), lambda i: (i, 0)),
        compiler_params=_params("parallel"),
        name="router",
    )(x, gain.reshape(1, d), router_w_pad)


def _moe_kernel(tile_e_ref, tile_rows_ref, tile_pos_ref, n_active_ref, tok_ref, x_hbm, g_ref,
                wg_ref, wu_ref, wd_ref, o_ref, x_scr, wg_scr, wu_scr, wd_scr, sem, *, tm, sub):
    del tile_e_ref, n_active_ref
    i = pl.program_id(0)
    j = pl.program_id(1)
    rows = tile_rows_ref[i]
    n_sub = (rows + sub - 1) // sub

    def row_copy(tok, r):
        return pltpu.make_async_copy(x_hbm.at[pl.ds(tok, 1)], o_ref.at[pl.ds(r, 1)], sem)

    @pl.when(rows > 0)
    def _():
        @pl.when(j == 0)
        def _():
            pos0 = tile_pos_ref[i]

            def issue(c, carry):
                for u in range(DMA_UNROLL):
                    r = c * DMA_UNROLL + u
                    row_copy(tok_ref[pos0 + r], r).start(priority=u % 2)
                return carry

            lax.fori_loop(0, n_sub * (sub // DMA_UNROLL), issue, 0)

            def wait(c, carry):
                for u in range(DMA_UNROLL):
                    row_copy(0, 0).wait()
                return carry

            lax.fori_loop(0, n_sub * (sub // DMA_UNROLL), wait, 0)
            for s in range(tm // sub):
                @pl.when(s < n_sub)
                def _():
                    rs = slice(s * sub, (s + 1) * sub)
                    x_scr[rs, :] = _rmsnorm_rows(o_ref[rs, :], g_ref[...]).astype(BF16)
                    o_ref[rs, :] = jnp.zeros((sub, o_ref.shape[1]), F32)

        def sub_block(s, cast_weights):
            rs = slice(s * sub, (s + 1) * sub)
            x = x_scr[rs, :]
            if cast_weights:
                wg_scr[...] = wg_ref[0].astype(BF16)
            g = _dot(x, wg_scr[...])
            if cast_weights:
                wu_scr[...] = wu_ref[0].astype(BF16)
            u = _dot(x, wu_scr[...])
            a = (g * jax.nn.sigmoid(g) * u).astype(BF16)
            if cast_weights:
                wd_scr[...] = wd_ref[0].astype(BF16)
            o_ref[rs, :] += _dot(a, wd_scr[...])

        sub_block(0, True)
        for s in range(1, tm // sub):
            @pl.when(s < n_sub)
            def _():
                sub_block(s, False)


def moe_experts(tile_e, tile_rows, tile_pos, n_active, tok_sorted, x, gain, w_gu, w_dn, *, tm, tf, sub):
    n_tiles = tile_e.shape[0]
    d = x.shape[1]
    f = w_dn.shape[1]
    nf = f // tf
    last = nf - 1

    def fchunk(i, j, rows):
        return jnp.where(rows[i] > 0, j, last)

    grid_spec = pltpu.PrefetchScalarGridSpec(
        num_scalar_prefetch=5,
        grid=(n_tiles, nf),
        in_specs=[
            pl.BlockSpec(memory_space=pl.ANY),
            pl.BlockSpec((1, d), lambda i, j, te, tr, tp, na, tk: (0, 0)),
            pl.BlockSpec((1, d, tf), lambda i, j, te, tr, tp, na, tk: (te[i], 0, fchunk(i, j, tr))),
            pl.BlockSpec((1, d, tf), lambda i, j, te, tr, tp, na, tk: (te[i], 0, nf + fchunk(i, j, tr))),
            pl.BlockSpec((1, tf, d), lambda i, j, te, tr, tp, na, tk: (te[i], fchunk(i, j, tr), 0)),
        ],
        out_specs=pl.BlockSpec((tm, d), lambda i, j, te, tr, tp, na, tk: (jnp.minimum(i, na[0] - 1), 0),
                               pipeline_mode=pl.Buffered(1)),
        scratch_shapes=[
            pltpu.VMEM((tm, d), BF16),
            pltpu.VMEM((d, tf), BF16),
            pltpu.VMEM((d, tf), BF16),
            pltpu.VMEM((tf, d), BF16),
            pltpu.SemaphoreType.DMA,
        ],
    )
    return pl.pallas_call(
        functools.partial(_moe_kernel, tm=tm, sub=sub),
        out_shape=jax.ShapeDtypeStruct((n_tiles * tm, d), F32),
        grid_spec=grid_spec,
        compiler_params=_params("arbitrary", "arbitrary"),
        name="moe_experts",
    )(tile_e, tile_rows, tile_pos, n_active, tok_sorted, x, gain.reshape(1, d), w_gu, w_gu, w_dn)


def _combine_kernel(slot_ref, ys_hbm, x_ref, gate_ref, o_ref, buf, sem, *, tm, tok0):
    i = pl.program_id(0)
    per_iter = max(1, DMA_UNROLL // TOP_K)

    def row_copy(slot, k, r):
        return pltpu.make_async_copy(ys_hbm.at[pl.ds(slot, 1)], buf.at[k, pl.ds(r, 1)], sem)

    def issue(c, carry):
        for u in range(per_iter):
            r = c * per_iter + u
            for k in range(TOP_K):
                row_copy(slot_ref[(tok0 + i * tm + r) * TOP_K + k], k, r).start(priority=k % 2)
        return carry

    lax.fori_loop(0, tm // per_iter, issue, 0)

    def wait(c, carry):
        for u in range(per_iter * TOP_K):
            row_copy(0, 0, 0).wait()
        return carry

    lax.fori_loop(0, tm // per_iter, wait, 0)
    gates = gate_ref[...]
    o_ref[...] = x_ref[...] + gates[:, 0:1] * buf[0] + gates[:, 1:2] * buf[1]


def moe_combine(slots_flat, ys, x, route, *, rows, tm, row_block0=0):
    d = x.shape[1]
    grid_spec = pltpu.PrefetchScalarGridSpec(
        num_scalar_prefetch=1,
        grid=(rows // tm,),
        in_specs=[
            pl.BlockSpec(memory_space=pl.ANY),
            pl.BlockSpec((tm, d), lambda i, s: (row_block0 + i, 0)),
            pl.BlockSpec((tm, LANES), lambda i, s: (i, 0)),
        ],
        out_specs=pl.BlockSpec((tm, d), lambda i, s: (i, 0)),
        scratch_shapes=[pltpu.VMEM((TOP_K, tm, d), F32), pltpu.SemaphoreType.DMA],
    )
    return pl.pallas_call(
        functools.partial(_combine_kernel, tm=tm, tok0=row_block0 * tm),
        out_shape=jax.ShapeDtypeStruct((rows, d), F32),
        grid_spec=grid_spec,
        compiler_params=_params("arbitrary"),
        name="moe_combine",
    )(slots_flat, ys, x, route)


def _routing_tables(route, *, tm, n_tiles):
    expert = route[:, 2:4].astype(jnp.int32)
    onehot = expert[:, :, None] == jnp.arange(N_EXPERTS, dtype=jnp.int32)[None, None, :]
    chosen = jnp.any(onehot, axis=1).astype(jnp.int32)
    before = jnp.cumsum(chosen, axis=0) - chosen
    counts = jnp.sum(chosen, axis=0)
    tiles_e = (counts + tm - 1) // tm
    tile_end = jnp.cumsum(tiles_e)
    tile_start = tile_end - tiles_e
    slot_te = tile_start[None, :] * tm + before
    slot_of_assign = jnp.sum(jnp.where(onehot, slot_te[:, None, :], 0), axis=-1).reshape(-1)
    order = jnp.argsort(expert.reshape(-1), stable=True).astype(jnp.int32)
    tok_sorted = jnp.pad(order // TOP_K, (0, tm))
    sorted_off = jnp.cumsum(counts) - counts
    tiles = jnp.arange(n_tiles, dtype=jnp.int32)
    n_active = tile_end[-1]
    te = jnp.minimum(jnp.searchsorted(tile_end, tiles, side="right"), N_EXPERTS - 1).astype(jnp.int32)
    local = (tiles - tile_start[te]) * tm
    active = tiles < n_active
    rows = jnp.where(active, jnp.clip(counts[te] - local, 0, tm), 0).astype(jnp.int32)
    pos = jnp.where(active, sorted_off[te] + local, 0).astype(jnp.int32)
    te = jnp.where(active, te, te[jnp.maximum(n_active - 1, 0)]).astype(jnp.int32)
    return te, rows, pos, n_active.reshape(1).astype(jnp.int32), tok_sorted, slot_of_assign.astype(jnp.int32)


def _row_tile(m, target):
    return target if m % target == 0 else m


def _cache_rows(cache):
    dbs, past, _, hgn, hd = cache.shape
    return jnp.transpose(cache, (0, 1, 3, 2, 4)).reshape(dbs, past * hgn * 2, hd)


def _state_from_rows(rows, hgn):
    bsz, n, hd = rows.shape
    past = n // (2 * hgn)
    return jnp.transpose(rows.reshape(bsz, past, hgn, 2, hd), (0, 1, 3, 2, 4))[None]


def kernel(x_prompt, x_sample, state_conv, cache_kv_w128, cache_kv_w512, cache_kv_w2048, norm_mix_w, norm_ffn_w, conv_in_w, conv_w, conv_out_w, attn_qkv_w, q_norm_w, k_norm_w, attn_out_w, ffn_gate_up_w, ffn_down_w, router_w, moe_gate_up_w, moe_down_w):
    bsz, seq, d = x_prompt.shape
    dbs, t_new, _ = x_sample.shape
    mp = bsz * seq
    ms = dbs * t_new
    xp = x_prompt.reshape(mp, d)
    xs = x_sample.reshape(ms, d)
    caches = (cache_kv_w128, cache_kv_w512, cache_kv_w2048)
    for gi in range(N_GROUPS):
        assert caches[gi].shape[2] == GROUP_WINDOWS[gi]
    assert CONV_WIDTH - 1 <= t_new <= SUBLANES and mp % ms == 0

    tmp = _row_tile(mp, 1024)

    b_p, u_p = conv_in_proj(xp, norm_mix_w[0], conv_in_w[0], tm=tmp, tn=256)
    b_s, u_s = conv_in_proj(xs, norm_mix_w[0], conv_in_w[0], tm=ms, tn=512, mxu=F32)
    z_p = conv_prompt(b_p, u_p, conv_w[0], seq=seq, tm=256)
    st = state_conv[0]
    prev1 = jnp.pad(st[:, 1:2], ((0, 0), (0, t_new - 1), (0, 0))).reshape(ms, d)
    prev2 = jnp.pad(st, ((0, 0), (0, t_new - 2), (0, 0))).reshape(ms, d)
    z_s = conv_sample(b_s, u_s, prev1, prev2, conv_w[0], t_new=t_new, out_dtype=F32)
    xp = matmul_res(z_p, conv_out_w[0], xp, tm=tmp, tn=512)
    xs = matmul_res(z_s, conv_out_w[0], xs, tm=ms, tn=512)
    new_conv_prompt = u_p.reshape(bsz, seq, d)[:, -(CONV_WIDTH - 1):][None]
    new_conv_sample = u_s.reshape(dbs, t_new, d)[:, -(CONV_WIDTH - 1):][None]

    a_p = ffn_up(xp, norm_ffn_w[0], ffn_gate_up_w[0], tm=tmp, tn=512)
    a_s = ffn_up(xs, norm_ffn_w[0], ffn_gate_up_w[0], tm=ms, tn=512, mxu=F32)
    xp = matmul_res(a_p, ffn_down_w[0], xp, tm=tmp, tn=256)
    xs = matmul_res(a_s, ffn_down_w[0], xs, tm=ms, tn=256)

    qkv_p = qkv_proj(xp, norm_mix_w[1], attn_qkv_w[0], q_norm_w[0], k_norm_w[0], tm=tmp, tn=512)
    qkv_s = qkv_proj(xs, norm_mix_w[1], attn_qkv_w[0], q_norm_w[0], k_norm_w[0], tm=ms, tn=512, mxu=F32)
    qkv_p3 = qkv_p.reshape(bsz, seq, 3 * d)
    qkv_s_pad = jnp.pad(qkv_s.reshape(dbs, t_new, 3 * d), ((0, 0), (0, SUBLANES - t_new), (0, 0)))

    outs_p, lses_p, outs_s, lses_s = [], [], [], []
    kv_prompt, kv_sample = [], []
    for gi in range(N_GROUPS):
        hgn = GROUP_HEADS[gi]
        o, lse = attn_prompt_group(qkv_p3, gi)
        outs_p.append(o.reshape(mp, hgn * HEAD_DIM))
        lses_p.append(lse.reshape(mp, LANES))
        o, lse, state = attn_sample_group(qkv_s_pad, _cache_rows(caches[gi][0]), gi, t_new=t_new)
        outs_s.append(o[:, :t_new].reshape(ms, hgn * HEAD_DIM))
        lses_s.append(lse[:, :t_new].reshape(ms, LANES))
        kv_sample.append(_state_from_rows(state, hgn))
        keep = min(GROUP_WINDOWS[gi], seq)
        kv_prompt.append(_state_from_rows(kv_state_prompt(qkv_p3, gi, keep=keep), hgn))

    merged_p = merge_groups(outs_p, lses_p, tm=_row_tile(mp, 512))
    merged_s = merge_groups(outs_s, lses_s, tm=ms, out_dtype=F32)
    x_all = matmul_res_joint(merged_p, merged_s, attn_out_w[0], xp, xs, tm=tmp, tn=512)

    rw_pad = jnp.pad(router_w[0], ((0, 0), (0, LANES - N_EXPERTS)))
    tmr = _row_tile(mp, 512)
    route_p = router(x_all, norm_ffn_w[1], rw_pad, rows=mp, tm=tmr)
    route_s = router(x_all, norm_ffn_w[1], rw_pad, rows=ms, tm=ms, row_block0=mp // ms, mxu=F32)
    route = jnp.concatenate([route_p, route_s], axis=0)
    tm_moe = 2304
    n_tiles = ((mp + ms) * TOP_K) // tm_moe + N_EXPERTS
    tile_e, tile_rows, tile_pos, n_active, tok_sorted, slot_of_assign = _routing_tables(
        route, tm=tm_moe, n_tiles=n_tiles)
    ys = moe_experts(tile_e, tile_rows, tile_pos, n_active, tok_sorted, x_all, norm_ffn_w[1],
                     moe_gate_up_w[0], moe_down_w[0], tm=tm_moe, tf=256, sub=256)
    tmc = _row_tile(mp, 256)
    y_p = moe_combine(slot_of_assign, ys, x_all, route_p, rows=mp, tm=tmc)
    y_s = moe_combine(slot_of_assign, ys, x_all, route_s, rows=ms, tm=ms, row_block0=mp // ms)

    return (y_p.reshape(bsz, seq, d), y_s.reshape(dbs, t_new, d), new_conv_prompt,
            kv_prompt[0], kv_prompt[1], kv_prompt[2], new_conv_sample,
            kv_sample[0], kv_sample[1], kv_sample[2])
```

```python
import functools
import math

import jax
import jax.numpy as jnp
from jax import lax
from jax.experimental import pallas as pl
from jax.experimental.pallas import tpu as pltpu

D_MODEL = 2048
N_HEADS = 16
HEAD_DIM = 128
GROUP_WINDOWS = (128, 512, 2048)
GROUP_DILATIONS = (1, 4, 16)
GROUP_HEADS = (6, 5, 5)
HEAD_OFFSETS = (0, 6, 11, 16)
N_GROUPS = 3
CONV_WIDTH = 3
N_EXPERTS = 8
TOP_K = 2
RMS_EPS = 1e-6
NEG_INF = -1e30

LANES = 128
SUBLANES = 8
ATTN_BLOCK = 128
ATTN_TILE = 2048
ATTN_GROUP = 4
VMEM_LIMIT = 56 * 1024 * 1024
DMA_UNROLL = 8

BF16 = jnp.bfloat16
F32 = jnp.float32


def _params(*sem):
    return pltpu.CompilerParams(dimension_semantics=sem, vmem_limit_bytes=VMEM_LIMIT)


def _dot(a, b, mxu=BF16):
    if mxu == F32:
        return jnp.dot(a, b, preferred_element_type=F32, precision=lax.Precision.HIGHEST)
    return jnp.dot(a.astype(BF16), b.astype(BF16), preferred_element_type=F32)


def _dot_nt(a, b, mxu=BF16):
    dims = (((1,), (1,)), ((), ()))
    if mxu == F32:
        return lax.dot_general(a, b, dims, preferred_element_type=F32, precision=lax.Precision.HIGHEST)
    return lax.dot_general(a.astype(BF16), b.astype(BF16), dims, preferred_element_type=F32)


def _rmsnorm_rows(x, gain):
    ms = jnp.mean(x * x, axis=-1, keepdims=True)
    return (x * lax.rsqrt(ms + RMS_EPS)) * gain


def _conv_in_kernel(x_ref, g_ref, wb_ref, wc_ref, wh_ref, b_ref, u_ref, h_scr):
    mxu = h_scr.dtype

    @pl.when(pl.program_id(1) == 0)
    def _():
        h_scr[...] = _rmsnorm_rows(x_ref[...], g_ref[...]).astype(mxu)

    h = h_scr[...]
    b_ref[...] = _dot(h, wb_ref[...], mxu)
    c = _dot(h, wc_ref[...], mxu)
    hx = _dot(h, wh_ref[...], mxu)
    u_ref[...] = c * hx


def conv_in_proj(x, gain, w, *, tm, tn, mxu=BF16):
    m, d = x.shape
    nj = d // tn
    return pl.pallas_call(
        _conv_in_kernel,
        out_shape=(jax.ShapeDtypeStruct((m, d), F32), jax.ShapeDtypeStruct((m, d), F32)),
        grid=(m // tm, nj),
        in_specs=[
            pl.BlockSpec((tm, d), lambda i, j: (i, 0)),
            pl.BlockSpec((1, d), lambda i, j: (0, 0)),
            pl.BlockSpec((d, tn), lambda i, j: (0, j)),
            pl.BlockSpec((d, tn), lambda i, j: (0, j + nj)),
            pl.BlockSpec((d, tn), lambda i, j: (0, j + 2 * nj)),
        ],
        out_specs=(pl.BlockSpec((tm, tn), lambda i, j: (i, j)),
                   pl.BlockSpec((tm, tn), lambda i, j: (i, j))),
        scratch_shapes=[pltpu.VMEM((tm, d), mxu)],
        compiler_params=_params("parallel", "arbitrary"),
        name="conv_in_proj",
    )(x, gain.reshape(1, d), w, w, w)


def _ffn_up_kernel(x_ref, g_ref, wg_ref, wu_ref, a_ref, h_scr):
    mxu = h_scr.dtype

    @pl.when(pl.program_id(1) == 0)
    def _():
        h_scr[...] = _rmsnorm_rows(x_ref[...], g_ref[...]).astype(mxu)

    h = h_scr[...]
    g = _dot(h, wg_ref[...], mxu)
    u = _dot(h, wu_ref[...], mxu)
    a_ref[...] = (g * jax.nn.sigmoid(g) * u).astype(mxu)


def ffn_up(x, gain, w, *, tm, tn, mxu=BF16):
    m, d = x.shape
    f = w.shape[1] // 2
    nj = f // tn
    return pl.pallas_call(
        _ffn_up_kernel,
        out_shape=jax.ShapeDtypeStruct((m, f), mxu),
        grid=(m // tm, nj),
        in_specs=[
            pl.BlockSpec((tm, d), lambda i, j: (i, 0)),
            pl.BlockSpec((1, d), lambda i, j: (0, 0)),
            pl.BlockSpec((d, tn), lambda i, j: (0, j)),
            pl.BlockSpec((d, tn), lambda i, j: (0, j + nj)),
        ],
        out_specs=pl.BlockSpec((tm, tn), lambda i, j: (i, j)),
        scratch_shapes=[pltpu.VMEM((tm, d), mxu)],
        compiler_params=_params("parallel", "arbitrary"),
        name="ffn_up",
    )(x, gain.reshape(1, d), w, w)


def _qkv_kernel(x_ref, g_ref, w_ref, qn_ref, kn_ref, o_ref, h_scr, *, tn, d_model):
    j = pl.program_id(1)
    mxu = h_scr.dtype

    @pl.when(j == 0)
    def _():
        h_scr[...] = _rmsnorm_rows(x_ref[...], g_ref[...]).astype(mxu)

    y = _dot(h_scr[...], w_ref[...], mxu)
    nq = d_model // tn
    is_qk = j < 2 * nq
    gain = jnp.where(is_qk, jnp.where(j < nq, qn_ref[...], kn_ref[...]), 1.0)
    for hh in range(tn // HEAD_DIM):
        cols = slice(hh * HEAD_DIM, (hh + 1) * HEAD_DIM)
        yh = y[:, cols]
        ms = jnp.mean(yh * yh, axis=-1, keepdims=True)
        o_ref[:, cols] = (yh * jnp.where(is_qk, lax.rsqrt(ms + RMS_EPS), 1.0)) * gain


def qkv_proj(x, gain, w, q_norm_w, k_norm_w, *, tm, tn, mxu=BF16):
    m, d = x.shape
    n = w.shape[1]
    return pl.pallas_call(
        functools.partial(_qkv_kernel, tn=tn, d_model=d),
        out_shape=jax.ShapeDtypeStruct((m, n), F32),
        grid=(m // tm, n // tn),
        in_specs=[
            pl.BlockSpec((tm, d), lambda i, j: (i, 0)),
            pl.BlockSpec((1, d), lambda i, j: (0, 0)),
            pl.BlockSpec((d, tn), lambda i, j: (0, j)),
            pl.BlockSpec((1, HEAD_DIM), lambda i, j: (0, 0)),
            pl.BlockSpec((1, HEAD_DIM), lambda i, j: (0, 0)),
        ],
        out_specs=pl.BlockSpec((tm, tn), lambda i, j: (i, j)),
        scratch_shapes=[pltpu.VMEM((tm, d), mxu)],
        compiler_params=_params("parallel", "arbitrary"),
        name="qkv_proj",
    )(x, gain.reshape(1, d), w, q_norm_w.reshape(1, HEAD_DIM), k_norm_w.reshape(1, HEAD_DIM))


def _matmul_res_kernel(a_ref, w_ref, r_ref, o_ref):
    o_ref[...] = r_ref[...] + _dot(a_ref[...], w_ref[...], a_ref.dtype)


def matmul_res(a, w, res, *, tm, tn):
    m, k = a.shape
    n = w.shape[1]
    return pl.pallas_call(
        _matmul_res_kernel,
        out_shape=jax.ShapeDtypeStruct((m, n), F32),
        grid=(m // tm, n // tn),
        in_specs=[
            pl.BlockSpec((tm, k), lambda i, j: (i, 0)),
            pl.BlockSpec((k, tn), lambda i, j: (0, j)),
            pl.BlockSpec((tm, tn), lambda i, j: (i, j)),
        ],
        out_specs=pl.BlockSpec((tm, tn), lambda i, j: (i, j)),
        compiler_params=_params("parallel", "arbitrary"),
        name="matmul_res",
    )(a, w, res)


def _matmul_res_joint_kernel(ap_ref, as_ref, w_ref, rp_ref, rs_ref, o_ref, *, n_prompt_tiles):
    i = pl.program_id(0)

    @pl.when(i < n_prompt_tiles)
    def _():
        o_ref[...] = rp_ref[...] + _dot(ap_ref[...], w_ref[...], ap_ref.dtype)

    @pl.when(i == n_prompt_tiles)
    def _():
        o_ref[0:as_ref.shape[0], :] = rs_ref[...] + _dot(as_ref[...], w_ref[...], as_ref.dtype)


def matmul_res_joint(a_p, a_s, w, res_p, res_s, *, tm, tn):
    mp, k = a_p.shape
    ms = a_s.shape[0]
    n = w.shape[1]
    nt = mp // tm
    return pl.pallas_call(
        functools.partial(_matmul_res_joint_kernel, n_prompt_tiles=nt),
        out_shape=jax.ShapeDtypeStruct((mp + ms, n), F32),
        grid=(nt + 1, n // tn),
        in_specs=[
            pl.BlockSpec((tm, k), lambda i, j: (jnp.minimum(i, nt - 1), 0)),
            pl.BlockSpec((ms, k), lambda i, j: (0, 0)),
            pl.BlockSpec((k, tn), lambda i, j: (0, j)),
            pl.BlockSpec((tm, tn), lambda i, j: (jnp.minimum(i, nt - 1), j)),
            pl.BlockSpec((ms, tn), lambda i, j: (0, j)),
        ],
        out_specs=pl.BlockSpec((tm, tn), lambda i, j: (i, j)),
        compiler_params=_params("arbitrary", "arbitrary"),
        name="matmul_res_joint",
    )(a_p, a_s, w, res_p, res_s)


def _gated_conv(b, u, um1, um2, cw):
    return b * (cw[0:1, :] * um2 + cw[1:2, :] * um1 + cw[2:3, :] * u)


def _conv_prompt_kernel(b_ref, u_ref, halo_ref, cw_ref, z_ref, *, tiles_per_seq):
    u = u_ref[...]
    halo = jnp.where(pl.program_id(0) % tiles_per_seq == 0, 0.0, halo_ref[...])
    h1 = halo[SUBLANES - 1:SUBLANES, :]
    h2 = halo[SUBLANES - 2:SUBLANES - 1, :]
    row = lax.broadcasted_iota(jnp.int32, u.shape, 0)
    um1 = jnp.where(row == 0, h1, pltpu.roll(u, 1, axis=0))
    um2 = jnp.where(row == 0, h2, jnp.where(row == 1, h1, pltpu.roll(u, 2, axis=0)))
    z_ref[...] = _gated_conv(b_ref[...], u, um1, um2, cw_ref[...]).astype(z_ref.dtype)


def conv_prompt(b, u, conv_w, *, seq, tm):
    m, d = u.shape
    halo_blocks = tm // SUBLANES
    return pl.pallas_call(
        functools.partial(_conv_prompt_kernel, tiles_per_seq=seq // tm),
        out_shape=jax.ShapeDtypeStruct((m, d), BF16),
        grid=(m // tm,),
        in_specs=[
            pl.BlockSpec((tm, d), lambda i: (i, 0)),
            pl.BlockSpec((tm, d), lambda i: (i, 0)),
            pl.BlockSpec((SUBLANES, d), lambda i: (jnp.maximum(i * halo_blocks - 1, 0), 0)),
            pl.BlockSpec((CONV_WIDTH, d), lambda i: (0, 0)),
        ],
        out_specs=pl.BlockSpec((tm, d), lambda i: (i, 0)),
        compiler_params=_params("parallel"),
        name="conv_prompt",
    )(b, u, u, conv_w)


def _conv_sample_kernel(b_ref, u_ref, p1_ref, p2_ref, cw_ref, z_ref, *, t_new):
    u = u_ref[...]
    t = lax.broadcasted_iota(jnp.int32, u.shape, 0) % t_new
    um1 = jnp.where(t >= 1, pltpu.roll(u, 1, axis=0), p1_ref[...])
    um2 = jnp.where(t >= 2, pltpu.roll(u, 2, axis=0), p2_ref[...])
    z_ref[...] = _gated_conv(b_ref[...], u, um1, um2, cw_ref[...]).astype(z_ref.dtype)


def conv_sample(b, u, prev1, prev2, conv_w, *, t_new, out_dtype):
    m, d = u.shape
    full = pl.BlockSpec((m, d), lambda i: (0, 0))
    return pl.pallas_call(
        functools.partial(_conv_sample_kernel, t_new=t_new),
        out_shape=jax.ShapeDtypeStruct((m, d), out_dtype),
        grid=(1,),
        in_specs=[full, full, full, full, pl.BlockSpec((CONV_WIDTH, d), lambda i: (0, 0))],
        out_specs=full,
        compiler_params=_params("arbitrary"),
        name="conv_sample",
    )(b, u, prev1, prev2, conv_w)


def _rows(start, size, stride):
    return pl.ds(start, size) if stride == 1 else pl.ds(start, size, stride=stride)


def _softmax_two_blocks(s_p, s_c, dtype=BF16):
    m = jnp.maximum(jnp.max(s_p, axis=-1, keepdims=True), jnp.max(s_c, axis=-1, keepdims=True))
    e_p = jnp.exp(s_p - m)
    e_c = jnp.exp(s_c - m)
    den = jnp.sum(e_p, axis=-1, keepdims=True) + jnp.sum(e_c, axis=-1, keepdims=True)
    inv = 1.0 / den
    return (e_p * inv).astype(dtype), (e_c * inv).astype(dtype), m + jnp.log(den)


def _attn_prompt_kernel(q_ref, k_ref, v_ref, kp_ref, vp_ref, o_ref, lse_ref, *, dil, tile):
    n = pl.program_id(1)
    hg = pl.program_id(2)
    blk = ATTN_BLOCK
    scale = HEAD_DIM ** -0.5

    @pl.when(hg == 0)
    def _():
        lse_ref[...] = jnp.zeros_like(lse_ref)

    qi = lax.broadcasted_iota(jnp.int32, (blk, blk), 0)
    ki = lax.broadcasted_iota(jnp.int32, (blk, blk), 1)
    lane = lax.broadcasted_iota(jnp.int32, (blk, LANES), 1)
    cur_mask = ki <= qi
    prev_band = ki >= qi
    first_mask = jnp.logical_and(prev_band, n > 0)
    blocks = [(r, qb) for r in range(dil) for qb in range(tile // (blk * dil))]
    for g0 in range(0, len(blocks), ATTN_GROUP):
        group = blocks[g0:g0 + ATTN_GROUP]
        rows, prev, scores, probs = [], [], [], []
        for r, qb in group:
            start = r + dil * blk * qb
            rows.append(_rows(start, blk, dil))
            if qb == 0:
                prev.append((kp_ref, vp_ref, _rows(r, blk, dil), first_mask))
            else:
                prev.append((k_ref, v_ref, _rows(start - dil * blk, blk, dil), prev_band))
        for rw, (kpr, _, prw, pmask) in zip(rows, prev):
            q = q_ref[0, rw, :]
            s_c = jnp.where(cur_mask, _dot_nt(q, k_ref[0, rw, :]) * scale, NEG_INF)
            s_p = jnp.where(pmask, _dot_nt(q, kpr[0, prw, :]) * scale, NEG_INF)
            scores.append((s_p, s_c))
        for s_p, s_c in scores:
            probs.append(_softmax_two_blocks(s_p, s_c))
        for rw, (_, vpr, prw, _), (p_p, p_c, _) in zip(rows, prev, probs):
            o_ref[0, rw, :] = _dot(p_p, vpr[0, prw, :]) + _dot(p_c, v_ref[0, rw, :])
        old = [lse_ref[0, rw, :] for rw in rows]
        for rw, o, (_, _, lse) in zip(rows, old, probs):
            lse_ref[0, rw, :] = jnp.where(lane == hg, lse, o)


def attn_prompt_group(qkv, gi):
    bsz, seq, _ = qkv.shape
    dil = GROUP_DILATIONS[gi]
    hgn = GROUP_HEADS[gi]
    lo = HEAD_OFFSETS[gi]
    tile = ATTN_TILE
    assert seq % tile == 0 and tile % (ATTN_BLOCK * dil) == 0
    assert GROUP_WINDOWS[gi] == ATTN_BLOCK * dil
    prev_rows = ATTN_BLOCK * dil
    ratio = tile // prev_rows

    def prev_map(col0):
        return lambda b, n, h: (b, jnp.maximum(n * ratio - 1, 0), col0 + lo + h)

    return pl.pallas_call(
        functools.partial(_attn_prompt_kernel, dil=dil, tile=tile),
        out_shape=(jax.ShapeDtypeStruct((bsz, seq, hgn * HEAD_DIM), F32),
                   jax.ShapeDtypeStruct((bsz, seq, LANES), F32)),
        grid=(bsz, seq // tile, hgn),
        in_specs=[
            pl.BlockSpec((1, tile, HEAD_DIM), lambda b, n, h: (b, n, lo + h)),
            pl.BlockSpec((1, tile, HEAD_DIM), lambda b, n, h: (b, n, N_HEADS + lo + h)),
            pl.BlockSpec((1, tile, HEAD_DIM), lambda b, n, h: (b, n, 2 * N_HEADS + lo + h)),
            pl.BlockSpec((1, prev_rows, HEAD_DIM), prev_map(N_HEADS)),
            pl.BlockSpec((1, prev_rows, HEAD_DIM), prev_map(2 * N_HEADS)),
        ],
        out_specs=(pl.BlockSpec((1, tile, HEAD_DIM), lambda b, n, h: (b, n, h)),
                   pl.BlockSpec((1, tile, LANES), lambda b, n, h: (b, n, 0))),
        compiler_params=_params("parallel", "parallel", "arbitrary"),
        name=f"attn_prompt_g{gi}",
    )(qkv, qkv, qkv, qkv, qkv)


def _kv_pack_kernel(*refs, hgn):
    o_ref = refs[-1]
    tp = refs[0].shape[1]
    for h in range(hgn):
        for kv in range(2):
            o_ref[0, pl.ds(2 * h + kv, tp, stride=2 * hgn), :] = refs[kv * hgn + h][0]


def kv_state_prompt(qkv, gi, *, keep):
    bsz, seq, _ = qkv.shape
    hgn = GROUP_HEADS[gi]
    lo = HEAD_OFFSETS[gi]
    tp = min(keep, 512)
    blk0 = (seq - keep) // tp
    specs = []
    for kv in range(2):
        for h in range(hgn):
            col = (1 + kv) * N_HEADS + lo + h
            specs.append(pl.BlockSpec((1, tp, HEAD_DIM), lambda b, p, col=col: (b, blk0 + p, col)))
    return pl.pallas_call(
        functools.partial(_kv_pack_kernel, hgn=hgn),
        out_shape=jax.ShapeDtypeStruct((bsz, keep * 2 * hgn, HEAD_DIM), F32),
        grid=(bsz, keep // tp),
        in_specs=specs,
        out_specs=pl.BlockSpec((1, tp * 2 * hgn, HEAD_DIM), lambda b, p: (b, p, 0)),
        compiler_params=_params("parallel", "parallel"),
        name=f"kv_state_prompt_g{gi}",
    )(*([qkv] * (2 * hgn)))


def _attn_sample_kernel(qkv_ref, c_ref, o_ref, lse_ref, st_ref, *, gi, t_new):
    dil = GROUP_DILATIONS[gi]
    window = GROUP_WINDOWS[gi]
    hgn = GROUP_HEADS[gi]
    lo = HEAD_OFFSETS[gi]
    per_pos = 2 * hgn
    rows = qkv_ref.shape[1]
    past = c_ref.shape[1] // per_pos
    scale = HEAD_DIM ** -0.5

    shift = t_new * per_pos
    kept = (past - t_new) * per_pos
    st_ref[0, 0:kept, :] = c_ref[0, shift:shift + kept, :]

    t_c = lax.broadcasted_iota(jnp.int32, (rows, past), 0)
    p_c = lax.broadcasted_iota(jnp.int32, (rows, past), 1)
    delta = past + t_c - p_c
    mask_c = jnp.logical_and(delta % dil == 0, delta <= window)
    t_n = lax.broadcasted_iota(jnp.int32, (rows, rows), 0)
    s_n = lax.broadcasted_iota(jnp.int32, (rows, rows), 1)
    dn = t_n - s_n
    mask_n = jnp.logical_and(jnp.logical_and(dn >= 0, dn % dil == 0),
                             jnp.logical_and(dn <= window, s_n < t_new))
    lane = lax.broadcasted_iota(jnp.int32, (rows, LANES), 1)
    lse_all = jnp.zeros((rows, LANES), F32)
    for h in range(hgn):
        def col(part):
            return slice((part * N_HEADS + lo + h) * HEAD_DIM, (part * N_HEADS + lo + h + 1) * HEAD_DIM)

        q = qkv_ref[0, :, col(0)]
        kn = qkv_ref[0, :, col(1)]
        vn = qkv_ref[0, :, col(2)]
        for t in range(t_new):
            row = kept + t * per_pos + 2 * h
            st_ref[0, row:row + 1, :] = kn[t:t + 1]
            st_ref[0, row + 1:row + 2, :] = vn[t:t + 1]
        kc = c_ref[0, pl.ds(2 * h, past, stride=per_pos), :]
        vc = c_ref[0, pl.ds(2 * h + 1, past, stride=per_pos), :]
        s_c = jnp.where(mask_c, _dot_nt(q, kc, F32) * scale, NEG_INF)
        s_new = jnp.where(mask_n, _dot_nt(q, kn, F32) * scale, NEG_INF)
        p_c, p_n, lse = _softmax_two_blocks(s_c, s_new, F32)
        o_ref[0, :, h * HEAD_DIM:(h + 1) * HEAD_DIM] = _dot(p_c, vc, F32) + _dot(p_n, vn, F32)
        lse_all = jnp.where(lane == h, lse, lse_all)
    lse_ref[0] = lse_all


def attn_sample_group(qkv_pad, cache, gi, *, t_new):
    dbs, rows, width = qkv_pad.shape
    crow = cache.shape[1]
    hgn = GROUP_HEADS[gi]
    assert (t_new * 2 * hgn) % SUBLANES == 0
    return pl.pallas_call(
        functools.partial(_attn_sample_kernel, gi=gi, t_new=t_new),
        out_shape=(jax.ShapeDtypeStruct((dbs, rows, hgn * HEAD_DIM), F32),
                   jax.ShapeDtypeStruct((dbs, rows, LANES), F32),
                   jax.ShapeDtypeStruct(cache.shape, F32)),
        grid=(dbs,),
        in_specs=[
            pl.BlockSpec((1, rows, width), lambda b: (b, 0, 0)),
            pl.BlockSpec((1, crow, HEAD_DIM), lambda b: (b, 0, 0)),
        ],
        out_specs=(pl.BlockSpec((1, rows, hgn * HEAD_DIM), lambda b: (b, 0, 0)),
                   pl.BlockSpec((1, rows, LANES), lambda b: (b, 0, 0)),
                   pl.BlockSpec((1, crow, HEAD_DIM), lambda b: (b, 0, 0))),
        compiler_params=_params("parallel"),
        name=f"attn_sample_g{gi}",
    )(qkv_pad, cache)


def _merge_kernel(o0_ref, o1_ref, o2_ref, l0_ref, l1_ref, l2_ref, m_ref):
    o_refs = (o0_ref, o1_ref, o2_ref)
    l_refs = (l0_ref, l1_ref, l2_ref)
    log_den = []
    for gi in range(N_GROUPS):
        lse = l_refs[gi][...]
        lane = lax.broadcasted_iota(jnp.int32, lse.shape, 1)
        valid = lane < GROUP_HEADS[gi]
        mx = jnp.max(jnp.where(valid, lse, -jnp.inf), axis=-1, keepdims=True)
        ssum = jnp.sum(jnp.where(valid, jnp.exp(lse - mx), 0.0), axis=-1, keepdims=True)
        log_den.append(mx + jnp.log(ssum) - math.log(GROUP_HEADS[gi]))
    top = jnp.maximum(jnp.maximum(log_den[0], log_den[1]), log_den[2])
    ex = [jnp.exp(ld - top) for ld in log_den]
    tot = ex[0] + ex[1] + ex[2]
    for gi in range(N_GROUPS):
        alpha = N_GROUPS * (ex[gi] / tot)
        cols = slice(HEAD_OFFSETS[gi] * HEAD_DIM, HEAD_OFFSETS[gi + 1] * HEAD_DIM)
        m_ref[:, cols] = (o_refs[gi][...] * alpha).astype(m_ref.dtype)


def merge_groups(outs, lses, *, tm, out_dtype=BF16):
    m = outs[0].shape[0]
    in_specs = [pl.BlockSpec((tm, o.shape[1]), lambda i: (i, 0)) for o in outs]
    in_specs += [pl.BlockSpec((tm, LANES), lambda i: (i, 0)) for _ in lses]
    return pl.pallas_call(
        _merge_kernel,
        out_shape=jax.ShapeDtypeStruct((m, N_HEADS * HEAD_DIM), out_dtype),
        grid=(m // tm,),
        in_specs=in_specs,
        out_specs=pl.BlockSpec((tm, N_HEADS * HEAD_DIM), lambda i: (i, 0)),
        compiler_params=_params("parallel"),
        name="merge_groups",
    )(*outs, *lses)


def _router_kernel(x_ref, g_ref, rw_ref, o_ref, *, mxu):
    logits = _dot(_rmsnorm_rows(x_ref[...], g_ref[...]), rw_ref[...], mxu)
    lane = lax.broadcasted_iota(jnp.int32, logits.shape, 1)
    valid = lane < N_EXPERTS
    mx = jnp.max(jnp.where(valid, logits, -jnp.inf), axis=-1, keepdims=True)
    ex = jnp.where(valid, jnp.exp(logits - mx), 0.0)
    probs = ex / jnp.sum(ex, axis=-1, keepdims=True)
    probs = jnp.where(valid, probs, -1.0)
    p1 = jnp.max(probs, axis=-1, keepdims=True)
    e1 = jnp.min(jnp.where(probs == p1, lane, LANES), axis=-1, keepdims=True)
    rest = jnp.where(lane == e1, -1.0, probs)
    p2 = jnp.max(rest, axis=-1, keepdims=True)
    e2 = jnp.min(jnp.where(rest == p2, lane, LANES), axis=-1, keepdims=True)
    tot = p1 + p2
    out = jnp.where(lane == 0, p1 / tot, 0.0)
    out = jnp.where(lane == 1, p2 / tot, out)
    out = jnp.where(lane == 2, e1.astype(F32), out)
    out = jnp.where(lane == 3, e2.astype(F32), out)
    o_ref[...] = out


def router(x, gain, router_w_pad, *, rows, tm, row_block0=0, mxu=BF16):
    d = x.shape[1]
    return pl.pallas_call(
        functools.partial(_router_kernel, mxu=mxu),
        out_shape=jax.ShapeDtypeStruct((rows, LANES), F32),
        grid=(rows // tm,),
        in_specs=[
            pl.BlockSpec((tm, d), lambda i: (row_block0 + i, 0)),
            pl.BlockSpec((1, d), lambda i: (0, 0)),
            pl.BlockSpec((d, LANES), lambda i: (0, 0)),
        ],
        out_specs=pl.BlockSpec((tm, LANES), lambda i: (i, 0)),
        compiler_params=_params("parallel"),
        name="router",
    )(x, gain.reshape(1, d), router_w_pad)


def _moe_kernel(tile_e_ref, tile_rows_ref, tile_pos_ref, n_active_ref, tok_ref, x_hbm, g_ref,
                wg_ref, wu_ref, wd_ref, o_ref, x_scr, wg_scr, wu_scr, wd_scr, a_scr, sem, *, tm, sub):
    del tile_e_ref, n_active_ref
    i = pl.program_id(0)
    j = pl.program_id(1)
    rows = tile_rows_ref[i]
    n_sub = (rows + sub - 1) // sub

    def row_copy(tok, r):
        return pltpu.make_async_copy(x_hbm.at[pl.ds(tok, 1)], o_ref.at[pl.ds(r, 1)], sem)

    @pl.when(rows > 0)
    def _():
        @pl.when(j == 0)
        def _():
            pos0 = tile_pos_ref[i]

            def issue(c, carry):
                for u in range(DMA_UNROLL):
                    r = c * DMA_UNROLL + u
                    row_copy(tok_ref[pos0 + r], r).start(priority=u % 2)
                return carry

            lax.fori_loop(0, n_sub * (sub // DMA_UNROLL), issue, 0)

            def wait(c, carry):
                for u in range(DMA_UNROLL):
                    row_copy(0, 0).wait()
                return carry

            lax.fori_loop(0, n_sub * (sub // DMA_UNROLL), wait, 0)
            for s in range(tm // sub):
                @pl.when(s < n_sub)
                def _():
                    rs = slice(s * sub, (s + 1) * sub)
                    x_scr[rs, :] = _rmsnorm_rows(o_ref[rs, :], g_ref[...]).astype(BF16)
                    o_ref[rs, :] = jnp.zeros((sub, o_ref.shape[1]), F32)

        def gate_up(s, cast_weights=False):
            x = x_scr[s * sub:(s + 1) * sub, :]
            if cast_weights:
                wg_scr[...] = wg_ref[0].astype(BF16)
            g = _dot(x, wg_scr[...])
            if cast_weights:
                wu_scr[...] = wu_ref[0].astype(BF16)
            u = _dot(x, wu_scr[...])
            a_scr[s % 2] = (g * jax.nn.sigmoid(g) * u).astype(BF16)

        def down(s):
            o_ref[s * sub:(s + 1) * sub, :] += _dot(a_scr[s % 2], wd_scr[...])

        gate_up(0, cast_weights=True)
        wd_scr[...] = wd_ref[0].astype(BF16)
        for s in range(1, tm // sub + 1):
            if s < tm // sub:
                @pl.when(s < n_sub)
                def _():
                    gate_up(s)
                    down(s - 1)

            @pl.when(s == n_sub)
            def _():
                down(s - 1)


def moe_experts(tile_e, tile_rows, tile_pos, n_active, tok_sorted, x, gain, w_gu, w_dn, *, tm, tf, sub):
    n_tiles = tile_e.shape[0]
    d = x.shape[1]
    f = w_dn.shape[1]
    nf = f // tf
    last = nf - 1

    def fchunk(i, j, rows):
        return jnp.where(rows[i] > 0, j, last)

    grid_spec = pltpu.PrefetchScalarGridSpec(
        num_scalar_prefetch=5,
        grid=(n_tiles, nf),
        in_specs=[
            pl.BlockSpec(memory_space=pl.ANY),
            pl.BlockSpec((1, d), lambda i, j, te, tr, tp, na, tk: (0, 0)),
            pl.BlockSpec((1, d, tf), lambda i, j, te, tr, tp, na, tk: (te[i], 0, fchunk(i, j, tr))),
            pl.BlockSpec((1, d, tf), lambda i, j, te, tr, tp, na, tk: (te[i], 0, nf + fchunk(i, j, tr))),
            pl.BlockSpec((1, tf, d), lambda i, j, te, tr, tp, na, tk: (te[i], fchunk(i, j, tr), 0)),
        ],
        out_specs=pl.BlockSpec((tm, d), lambda i, j, te, tr, tp, na, tk: (jnp.minimum(i, na[0] - 1), 0),
                               pipeline_mode=pl.Buffered(1)),
        scratch_shapes=[
            pltpu.VMEM((tm, d), BF16),
            pltpu.VMEM((d, tf), BF16),
            pltpu.VMEM((d, tf), BF16),
            pltpu.VMEM((tf, d), BF16),
            pltpu.VMEM((2, sub, tf), BF16),
            pltpu.SemaphoreType.DMA,
        ],
    )
    return pl.pallas_call(
        functools.partial(_moe_kernel, tm=tm, sub=sub),
        out_shape=jax.ShapeDtypeStruct((n_tiles * tm, d), F32),
        grid_spec=grid_spec,
        compiler_params=_params("arbitrary", "arbitrary"),
        name="moe_experts",
    )(tile_e, tile_rows, tile_pos, n_active, tok_sorted, x, gain.reshape(1, d), w_gu, w_gu, w_dn)


def _combine_kernel(slot_ref, ys_hbm, x_ref, gate_ref, o_ref, buf, sem, *, tm, tok0):
    i = pl.program_id(0)
    per_iter = max(1, DMA_UNROLL // TOP_K)

    def row_copy(slot, k, r):
        return pltpu.make_async_copy(ys_hbm.at[pl.ds(slot, 1)], buf.at[k, pl.ds(r, 1)], sem)

    def issue(c, carry):
        for u in range(per_iter):
            r = c * per_iter + u
            for k in range(TOP_K):
                row_copy(slot_ref[(tok0 + i * tm + r) * TOP_K + k], k, r).start(priority=k % 2)
        return carry

    lax.fori_loop(0, tm // per_iter, issue, 0)

    def wait(c, carry):
        for u in range(per_iter * TOP_K):
            row_copy(0, 0, 0).wait()
        return carry

    lax.fori_loop(0, tm // per_iter, wait, 0)
    gates = gate_ref[...]
    o_ref[...] = x_ref[...] + gates[:, 0:1] * buf[0] + gates[:, 1:2] * buf[1]


def moe_combine(slots_flat, ys, x, route, *, rows, tm, row_block0=0):
    d = x.shape[1]
    grid_spec = pltpu.PrefetchScalarGridSpec(
        num_scalar_prefetch=1,
        grid=(rows // tm,),
        in_specs=[
            pl.BlockSpec(memory_space=pl.ANY),
            pl.BlockSpec((tm, d), lambda i, s: (row_block0 + i, 0)),
            pl.BlockSpec((tm, LANES), lambda i, s: (i, 0)),
        ],
        out_specs=pl.BlockSpec((tm, d), lambda i, s: (i, 0)),
        scratch_shapes=[pltpu.VMEM((TOP_K, tm, d), F32), pltpu.SemaphoreType.DMA],
    )
    return pl.pallas_call(
        functools.partial(_combine_kernel, tm=tm, tok0=row_block0 * tm),
        out_shape=jax.ShapeDtypeStruct((rows, d), F32),
        grid_spec=grid_spec,
        compiler_params=_params("arbitrary"),
        name="moe_combine",
    )(slots_flat, ys, x, route)


def _routing_tables(route, *, tm, n_tiles):
    expert = route[:, 2:4].astype(jnp.int32)
    onehot = expert[:, :, None] == jnp.arange(N_EXPERTS, dtype=jnp.int32)[None, None, :]
    chosen = jnp.any(onehot, axis=1).astype(jnp.int32)
    before = jnp.cumsum(chosen, axis=0) - chosen
    counts = jnp.sum(chosen, axis=0)
    tiles_e = (counts + tm - 1) // tm
    tile_end = jnp.cumsum(tiles_e)
    tile_start = tile_end - tiles_e
    slot_te = tile_start[None, :] * tm + before
    slot_of_assign = jnp.sum(jnp.where(onehot, slot_te[:, None, :], 0), axis=-1).reshape(-1)
    order = jnp.argsort(expert.reshape(-1), stable=True).astype(jnp.int32)
    tok_sorted = jnp.pad(order // TOP_K, (0, tm))
    sorted_off = jnp.cumsum(counts) - counts
    tiles = jnp.arange(n_tiles, dtype=jnp.int32)
    n_active = tile_end[-1]
    te = jnp.minimum(jnp.searchsorted(tile_end, tiles, side="right"), N_EXPERTS - 1).astype(jnp.int32)
    local = (tiles - tile_start[te]) * tm
    active = tiles < n_active
    rows = jnp.where(active, jnp.clip(counts[te] - local, 0, tm), 0).astype(jnp.int32)
    pos = jnp.where(active, sorted_off[te] + local, 0).astype(jnp.int32)
    te = jnp.where(active, te, te[jnp.maximum(n_active - 1, 0)]).astype(jnp.int32)
    return te, rows, pos, n_active.reshape(1).astype(jnp.int32), tok_sorted, slot_of_assign.astype(jnp.int32)


def _row_tile(m, target):
    return target if m % target == 0 else m


def _cache_rows(cache):
    dbs, past, _, hgn, hd = cache.shape
    return jnp.transpose(cache, (0, 1, 3, 2, 4)).reshape(dbs, past * hgn * 2, hd)


def _state_from_rows(rows, hgn):
    bsz, n, hd = rows.shape
    past = n // (2 * hgn)
    return jnp.transpose(rows.reshape(bsz, past, hgn, 2, hd), (0, 1, 3, 2, 4))[None]


def kernel(x_prompt, x_sample, state_conv, cache_kv_w128, cache_kv_w512, cache_kv_w2048, norm_mix_w, norm_ffn_w, conv_in_w, conv_w, conv_out_w, attn_qkv_w, q_norm_w, k_norm_w, attn_out_w, ffn_gate_up_w, ffn_down_w, router_w, moe_gate_up_w, moe_down_w):
    bsz, seq, d = x_prompt.shape
    dbs, t_new, _ = x_sample.shape
    mp = bsz * seq
    ms = dbs * t_new
    xp = x_prompt.reshape(mp, d)
    xs = x_sample.reshape(ms, d)
    caches = (cache_kv_w128, cache_kv_w512, cache_kv_w2048)
    for gi in range(N_GROUPS):
        assert caches[gi].shape[2] == GROUP_WINDOWS[gi]
    assert CONV_WIDTH - 1 <= t_new <= SUBLANES and mp % ms == 0

    tmp = _row_tile(mp, 1024)

    b_p, u_p = conv_in_proj(xp, norm_mix_w[0], conv_in_w[0], tm=tmp, tn=256)
    b_s, u_s = conv_in_proj(xs, norm_mix_w[0], conv_in_w[0], tm=ms, tn=512, mxu=F32)
    z_p = conv_prompt(b_p, u_p, conv_w[0], seq=seq, tm=256)
    st = state_conv[0]
    prev1 = jnp.pad(st[:, 1:2], ((0, 0), (0, t_new - 1), (0, 0))).reshape(ms, d)
    prev2 = jnp.pad(st, ((0, 0), (0, t_new - 2), (0, 0))).reshape(ms, d)
    z_s = conv_sample(b_s, u_s, prev1, prev2, conv_w[0], t_new=t_new, out_dtype=F32)
    xp = matmul_res(z_p, conv_out_w[0], xp, tm=tmp, tn=512)
    xs = matmul_res(z_s, conv_out_w[0], xs, tm=ms, tn=512)
    new_conv_prompt = u_p.reshape(bsz, seq, d)[:, -(CONV_WIDTH - 1):][None]
    new_conv_sample = u_s.reshape(dbs, t_new, d)[:, -(CONV_WIDTH - 1):][None]

    a_p = ffn_up(xp, norm_ffn_w[0], ffn_gate_up_w[0], tm=tmp, tn=512)
    a_s = ffn_up(xs, norm_ffn_w[0], ffn_gate_up_w[0], tm=ms, tn=512, mxu=F32)
    xp = matmul_res(a_p, ffn_down_w[0], xp, tm=tmp, tn=256)
    xs = matmul_res(a_s, ffn_down_w[0], xs, tm=ms, tn=256)

    qkv_p = qkv_proj(xp, norm_mix_w[1], attn_qkv_w[0], q_norm_w[0], k_norm_w[0], tm=tmp, tn=512)
    qkv_s = qkv_proj(xs, norm_mix_w[1], attn_qkv_w[0], q_norm_w[0], k_norm_w[0], tm=ms, tn=512, mxu=F32)
    qkv_p3 = qkv_p.reshape(bsz, seq, 3 * d)
    qkv_s_pad = jnp.pad(qkv_s.reshape(dbs, t_new, 3 * d), ((0, 0), (0, SUBLANES - t_new), (0, 0)))

    outs_p, lses_p, outs_s, lses_s = [], [], [], []
    kv_prompt, kv_sample = [], []
    for gi in range(N_GROUPS):
        hgn = GROUP_HEADS[gi]
        o, lse = attn_prompt_group(qkv_p3, gi)
        outs_p.append(o.reshape(mp, hgn * HEAD_DIM))
        lses_p.append(lse.reshape(mp, LANES))
        o, lse, state = attn_sample_group(qkv_s_pad, _cache_rows(caches[gi][0]), gi, t_new=t_new)
        outs_s.append(o[:, :t_new].reshape(ms, hgn * HEAD_DIM))
        lses_s.append(lse[:, :t_new].reshape(ms, LANES))
        kv_sample.append(_state_from_rows(state, hgn))
        keep = min(GROUP_WINDOWS[gi], seq)
        kv_prompt.append(_state_from_rows(kv_state_prompt(qkv_p3, gi, keep=keep), hgn))

    merged_p = merge_groups(outs_p, lses_p, tm=_row_tile(mp, 512))
    merged_s = merge_groups(outs_s, lses_s, tm=ms, out_dtype=F32)
    x_all = matmul_res_joint(merged_p, merged_s, attn_out_w[0], xp, xs, tm=tmp, tn=512)

    rw_pad = jnp.pad(router_w[0], ((0, 0), (0, LANES - N_EXPERTS)))
    tmr = _row_tile(mp, 512)
    route_p = router(x_all, norm_ffn_w[1], rw_pad, rows=mp, tm=tmr)
    route_s = router(x_all, norm_ffn_w[1], rw_pad, rows=ms, tm=ms, row_block0=mp // ms, mxu=F32)
    route = jnp.concatenate([route_p, route_s], axis=0)
    tm_moe = 2304
    n_tiles = ((mp + ms) * TOP_K) // tm_moe + N_EXPERTS
    tile_e, tile_rows, tile_pos, n_active, tok_sorted, slot_of_assign = _routing_tables(
        route, tm=tm_moe, n_tiles=n_tiles)
    ys = moe_experts(tile_e, tile_rows, tile_pos, n_active, tok_sorted, x_all, norm_ffn_w[1],
                     moe_gate_up_w[0], moe_down_w[0], tm=tm_moe, tf=256, sub=256)
    tmc = _row_tile(mp, 256)
    y_p = moe_combine(slot_of_assign, ys, x_all, route_p, rows=mp, tm=tmc)
    y_s = moe_combine(slot_of_assign, ys, x_all, route_s, rows=ms, tm=ms, row_block0=mp // ms)

    return (y_p.reshape(bsz, seq, d), y_s.reshape(dbs, t_new, d), new_conv_prompt,
            kv_prompt[0], kv_prompt[1], kv_prompt[2], new_conv_sample,
            kv_sample[0], kv_sample[1], kv_sample[2])
```

```python
import functools
import math

import jax
import jax.numpy as jnp
from jax import lax
from jax.experimental import pallas as pl
from jax.experimental.pallas import tpu as pltpu

D_MODEL = 2048
N_HEADS = 16
HEAD_DIM = 128
GROUP_WINDOWS = (128, 512, 2048)
GROUP_DILATIONS = (1, 4, 16)
GROUP_HEADS = (6, 5, 5)
HEAD_OFFSETS = (0, 6, 11, 16)
N_GROUPS = 3
CONV_WIDTH = 3
N_EXPERTS = 8
TOP_K = 2
RMS_EPS = 1e-6
NEG_INF = -1e30

LANES = 128
SUBLANES = 8
ATTN_BLOCK = 128
ATTN_TILE = 2048
ATTN_GROUP = 4
VMEM_LIMIT = 56 * 1024 * 1024
DMA_UNROLL = 8

BF16 = jnp.bfloat16
F32 = jnp.float32


def _params(*sem):
    return pltpu.CompilerParams(dimension_semantics=sem, vmem_limit_bytes=VMEM_LIMIT)


def _dot_general(a, b, dims, mxu):
    if mxu == F32:
        return lax.dot_general(a, b, dims, preferred_element_type=F32, precision=lax.Precision.HIGHEST)
    return lax.dot_general(a.astype(BF16), b.astype(BF16), dims, preferred_element_type=F32)


def _dot(a, b, mxu=BF16):
    return _dot_general(a, b, (((1,), (0,)), ((), ())), mxu)


def _dot_nt(a, b, mxu=BF16):
    return _dot_general(a, b, (((1,), (1,)), ((), ())), mxu)


def _rmsnorm_rows(x, gain):
    ms = jnp.mean(x * x, axis=-1, keepdims=True)
    return (x * lax.rsqrt(ms + RMS_EPS)) * gain


def _conv_in_kernel(x_ref, g_ref, wb_ref, wc_ref, wh_ref, b_ref, u_ref, h_scr):
    mxu = h_scr.dtype

    @pl.when(pl.program_id(1) == 0)
    def _():
        h_scr[...] = _rmsnorm_rows(x_ref[...], g_ref[...]).astype(mxu)

    h = h_scr[...]
    b_ref[...] = _dot(h, wb_ref[...], mxu)
    c = _dot(h, wc_ref[...], mxu)
    hx = _dot(h, wh_ref[...], mxu)
    u_ref[...] = c * hx


def conv_in_proj(x, gain, w, *, tm, tn, mxu=BF16):
    m, d = x.shape
    nj = d // tn
    return pl.pallas_call(
        _conv_in_kernel,
        out_shape=(jax.ShapeDtypeStruct((m, d), F32), jax.ShapeDtypeStruct((m, d), F32)),
        grid=(m // tm, nj),
        in_specs=[
            pl.BlockSpec((tm, d), lambda i, j: (i, 0)),
            pl.BlockSpec((1, d), lambda i, j: (0, 0)),
            pl.BlockSpec((d, tn), lambda i, j: (0, j)),
            pl.BlockSpec((d, tn), lambda i, j: (0, j + nj)),
            pl.BlockSpec((d, tn), lambda i, j: (0, j + 2 * nj)),
        ],
        out_specs=(pl.BlockSpec((tm, tn), lambda i, j: (i, j)),
                   pl.BlockSpec((tm, tn), lambda i, j: (i, j))),
        scratch_shapes=[pltpu.VMEM((tm, d), mxu)],
        compiler_params=_params("parallel", "arbitrary"),
        name="conv_in_proj",
    )(x, gain.reshape(1, d), w, w, w)


def _ffn_up_kernel(x_ref, g_ref, wg_ref, wu_ref, a_ref, h_scr):
    mxu = h_scr.dtype

    @pl.when(pl.program_id(1) == 0)
    def _():
        h_scr[...] = _rmsnorm_rows(x_ref[...], g_ref[...]).astype(mxu)

    h = h_scr[...]
    g = _dot(h, wg_ref[...], mxu)
    u = _dot(h, wu_ref[...], mxu)
    a_ref[...] = (g * jax.nn.sigmoid(g) * u).astype(mxu)


def ffn_up(x, gain, w, *, tm, tn, mxu=BF16):
    m, d = x.shape
    f = w.shape[1] // 2
    nj = f // tn
    return pl.pallas_call(
        _ffn_up_kernel,
        out_shape=jax.ShapeDtypeStruct((m, f), mxu),
        grid=(m // tm, nj),
        in_specs=[
            pl.BlockSpec((tm, d), lambda i, j: (i, 0)),
            pl.BlockSpec((1, d), lambda i, j: (0, 0)),
            pl.BlockSpec((d, tn), lambda i, j: (0, j)),
            pl.BlockSpec((d, tn), lambda i, j: (0, j + nj)),
        ],
        out_specs=pl.BlockSpec((tm, tn), lambda i, j: (i, j)),
        scratch_shapes=[pltpu.VMEM((tm, d), mxu)],
        compiler_params=_params("parallel", "arbitrary"),
        name="ffn_up",
    )(x, gain.reshape(1, d), w, w)


def _qkv_kernel(x_ref, g_ref, w_ref, qn_ref, kn_ref, o_ref, h_scr, *, tn, d_model):
    j = pl.program_id(1)
    mxu = h_scr.dtype

    @pl.when(j == 0)
    def _():
        h_scr[...] = _rmsnorm_rows(x_ref[...], g_ref[...]).astype(mxu)

    y = _dot(h_scr[...], w_ref[...], mxu)
    nq = d_model // tn
    is_qk = j < 2 * nq
    gain = jnp.where(is_qk, jnp.where(j < nq, qn_ref[...], kn_ref[...]), 1.0)
    for hh in range(tn // HEAD_DIM):
        cols = slice(hh * HEAD_DIM, (hh + 1) * HEAD_DIM)
        yh = y[:, cols]
        ms = jnp.mean(yh * yh, axis=-1, keepdims=True)
        o_ref[:, cols] = (yh * jnp.where(is_qk, lax.rsqrt(ms + RMS_EPS), 1.0)) * gain


def qkv_proj(x, gain, w, q_norm_w, k_norm_w, *, tm, tn, mxu=BF16):
    m, d = x.shape
    n = w.shape[1]
    return pl.pallas_call(
        functools.partial(_qkv_kernel, tn=tn, d_model=d),
        out_shape=jax.ShapeDtypeStruct((m, n), F32),
        grid=(m // tm, n // tn),
        in_specs=[
            pl.BlockSpec((tm, d), lambda i, j: (i, 0)),
            pl.BlockSpec((1, d), lambda i, j: (0, 0)),
            pl.BlockSpec((d, tn), lambda i, j: (0, j)),
            pl.BlockSpec((1, HEAD_DIM), lambda i, j: (0, 0)),
            pl.BlockSpec((1, HEAD_DIM), lambda i, j: (0, 0)),
        ],
        out_specs=pl.BlockSpec((tm, tn), lambda i, j: (i, j)),
        scratch_shapes=[pltpu.VMEM((tm, d), mxu)],
        compiler_params=_params("parallel", "arbitrary"),
        name="qkv_proj",
    )(x, gain.reshape(1, d), w, q_norm_w.reshape(1, HEAD_DIM), k_norm_w.reshape(1, HEAD_DIM))


def _matmul_res_kernel(a_ref, w_ref, r_ref, o_ref):
    o_ref[...] = r_ref[...] + _dot(a_ref[...], w_ref[...], a_ref.dtype)


def matmul_res(a, w, res, *, tm, tn):
    m, k = a.shape
    n = w.shape[1]
    return pl.pallas_call(
        _matmul_res_kernel,
        out_shape=jax.ShapeDtypeStruct((m, n), F32),
        grid=(m // tm, n // tn),
        in_specs=[
            pl.BlockSpec((tm, k), lambda i, j: (i, 0)),
            pl.BlockSpec((k, tn), lambda i, j: (0, j)),
            pl.BlockSpec((tm, tn), lambda i, j: (i, j)),
        ],
        out_specs=pl.BlockSpec((tm, tn), lambda i, j: (i, j)),
        compiler_params=_params("parallel", "arbitrary"),
        name="matmul_res",
    )(a, w, res)


def _matmul_res_joint_kernel(ap_ref, as_ref, w_ref, rp_ref, rs_ref, o_ref, *, n_prompt_tiles):
    i = pl.program_id(0)

    @pl.when(i < n_prompt_tiles)
    def _():
        o_ref[...] = rp_ref[...] + _dot(ap_ref[...], w_ref[...], ap_ref.dtype)

    @pl.when(i == n_prompt_tiles)
    def _():
        o_ref[0:as_ref.shape[0], :] = rs_ref[...] + _dot(as_ref[...], w_ref[...], as_ref.dtype)


def matmul_res_joint(a_p, a_s, w, res_p, res_s, *, tm, tn):
    mp, k = a_p.shape
    ms = a_s.shape[0]
    n = w.shape[1]
    nt = mp // tm
    return pl.pallas_call(
        functools.partial(_matmul_res_joint_kernel, n_prompt_tiles=nt),
        out_shape=jax.ShapeDtypeStruct((mp + ms, n), F32),
        grid=(nt + 1, n // tn),
        in_specs=[
            pl.BlockSpec((tm, k), lambda i, j: (jnp.minimum(i, nt - 1), 0)),
            pl.BlockSpec((ms, k), lambda i, j: (0, 0)),
            pl.BlockSpec((k, tn), lambda i, j: (0, j)),
            pl.BlockSpec((tm, tn), lambda i, j: (jnp.minimum(i, nt - 1), j)),
            pl.BlockSpec((ms, tn), lambda i, j: (0, j)),
        ],
        out_specs=pl.BlockSpec((tm, tn), lambda i, j: (i, j)),
        compiler_params=_params("arbitrary", "arbitrary"),
        name="matmul_res_joint",
    )(a_p, a_s, w, res_p, res_s)


def _gated_conv(b, u, um1, um2, cw):
    return b * (cw[0:1, :] * um2 + cw[1:2, :] * um1 + cw[2:3, :] * u)


def _conv_prompt_kernel(b_ref, u_ref, halo_ref, cw_ref, z_ref, *, tiles_per_seq):
    u = u_ref[...]
    halo = jnp.where(pl.program_id(0) % tiles_per_seq == 0, 0.0, halo_ref[...])
    h1 = halo[SUBLANES - 1:SUBLANES, :]
    h2 = halo[SUBLANES - 2:SUBLANES - 1, :]
    row = lax.broadcasted_iota(jnp.int32, u.shape, 0)
    um1 = jnp.where(row == 0, h1, pltpu.roll(u, 1, axis=0))
    um2 = jnp.where(row == 0, h2, jnp.where(row == 1, h1, pltpu.roll(u, 2, axis=0)))
    z_ref[...] = _gated_conv(b_ref[...], u, um1, um2, cw_ref[...]).astype(z_ref.dtype)


def conv_prompt(b, u, conv_w, *, seq, tm):
    m, d = u.shape
    halo_blocks = tm // SUBLANES
    return pl.pallas_call(
        functools.partial(_conv_prompt_kernel, tiles_per_seq=seq // tm),
        out_shape=jax.ShapeDtypeStruct((m, d), BF16),
        grid=(m // tm,),
        in_specs=[
            pl.BlockSpec((tm, d), lambda i: (i, 0)),
            pl.BlockSpec((tm, d), lambda i: (i, 0)),
            pl.BlockSpec((SUBLANES, d), lambda i: (jnp.maximum(i * halo_blocks - 1, 0), 0)),
            pl.BlockSpec((CONV_WIDTH, d), lambda i: (0, 0)),
        ],
        out_specs=pl.BlockSpec((tm, d), lambda i: (i, 0)),
        compiler_params=_params("parallel"),
        name="conv_prompt",
    )(b, u, u, conv_w)


def _conv_sample_kernel(b_ref, u_ref, p1_ref, p2_ref, cw_ref, z_ref, *, t_new):
    u = u_ref[...]
    t = lax.broadcasted_iota(jnp.int32, u.shape, 0) % t_new
    um1 = jnp.where(t >= 1, pltpu.roll(u, 1, axis=0), p1_ref[...])
    um2 = jnp.where(t >= 2, pltpu.roll(u, 2, axis=0), p2_ref[...])
    z_ref[...] = _gated_conv(b_ref[...], u, um1, um2, cw_ref[...]).astype(z_ref.dtype)


def conv_sample(b, u, prev1, prev2, conv_w, *, t_new, out_dtype):
    m, d = u.shape
    full = pl.BlockSpec((m, d), lambda i: (0, 0))
    return pl.pallas_call(
        functools.partial(_conv_sample_kernel, t_new=t_new),
        out_shape=jax.ShapeDtypeStruct((m, d), out_dtype),
        grid=(1,),
        in_specs=[full, full, full, full, pl.BlockSpec((CONV_WIDTH, d), lambda i: (0, 0))],
        out_specs=full,
        compiler_params=_params("arbitrary"),
        name="conv_sample",
    )(b, u, prev1, prev2, conv_w)


def _rows(start, size, stride):
    return pl.ds(start, size) if stride == 1 else pl.ds(start, size, stride=stride)


def _softmax_two_blocks(s_p, s_c, dtype=BF16):
    m = jnp.maximum(jnp.max(s_p, axis=-1, keepdims=True), jnp.max(s_c, axis=-1, keepdims=True))
    e_p = jnp.exp(s_p - m)
    e_c = jnp.exp(s_c - m)
    den = jnp.sum(e_p, axis=-1, keepdims=True) + jnp.sum(e_c, axis=-1, keepdims=True)
    inv = 1.0 / den
    return (e_p * inv).astype(dtype), (e_c * inv).astype(dtype), m + jnp.log(den)


def _attn_prompt_kernel(q_ref, k_ref, v_ref, kp_ref, vp_ref, o_ref, lse_ref, *, dil, tile):
    n = pl.program_id(1)
    hg = pl.program_id(2)
    blk = ATTN_BLOCK
    scale = HEAD_DIM ** -0.5

    @pl.when(hg == 0)
    def _():
        lse_ref[...] = jnp.zeros_like(lse_ref)

    qi = lax.broadcasted_iota(jnp.int32, (blk, blk), 0)
    ki = lax.broadcasted_iota(jnp.int32, (blk, blk), 1)
    lane = lax.broadcasted_iota(jnp.int32, (blk, LANES), 1)
    cur_mask = ki <= qi
    prev_band = ki >= qi
    first_mask = jnp.logical_and(prev_band, n > 0)
    blocks = [(r, qb) for r in range(dil) for qb in range(tile // (blk * dil))]
    for g0 in range(0, len(blocks), ATTN_GROUP):
        group = blocks[g0:g0 + ATTN_GROUP]
        rows, prev, scores, probs = [], [], [], []
        for r, qb in group:
            start = r + dil * blk * qb
            rows.append(_rows(start, blk, dil))
            if qb == 0:
                prev.append((kp_ref, vp_ref, _rows(r, blk, dil), first_mask))
            else:
                prev.append((k_ref, v_ref, _rows(start - dil * blk, blk, dil), prev_band))
        for rw, (kpr, _, prw, pmask) in zip(rows, prev):
            q = q_ref[0, rw, :]
            s_c = jnp.where(cur_mask, _dot_nt(q, k_ref[0, rw, :]) * scale, NEG_INF)
            s_p = jnp.where(pmask, _dot_nt(q, kpr[0, prw, :]) * scale, NEG_INF)
            scores.append((s_p, s_c))
        for s_p, s_c in scores:
            probs.append(_softmax_two_blocks(s_p, s_c))
        for rw, (_, vpr, prw, _), (p_p, p_c, _) in zip(rows, prev, probs):
            o_ref[0, rw, :] = _dot(p_p, vpr[0, prw, :]) + _dot(p_c, v_ref[0, rw, :])
        old = [lse_ref[0, rw, :] for rw in rows]
        for rw, o, (_, _, lse) in zip(rows, old, probs):
            lse_ref[0, rw, :] = jnp.where(lane == hg, lse, o)


def attn_prompt_group(qkv, gi):
    bsz, seq, _ = qkv.shape
    dil = GROUP_DILATIONS[gi]
    hgn = GROUP_HEADS[gi]
    lo = HEAD_OFFSETS[gi]
    tile = ATTN_TILE
    assert seq % tile == 0 and tile % (ATTN_BLOCK * dil) == 0
    assert GROUP_WINDOWS[gi] == ATTN_BLOCK * dil
    prev_rows = ATTN_BLOCK * dil
    ratio = tile // prev_rows

    def prev_map(col0):
        return lambda b, n, h: (b, jnp.maximum(n * ratio - 1, 0), col0 + lo + h)

    return pl.pallas_call(
        functools.partial(_attn_prompt_kernel, dil=dil, tile=tile),
        out_shape=(jax.ShapeDtypeStruct((bsz, seq, hgn * HEAD_DIM), F32),
                   jax.ShapeDtypeStruct((bsz, seq, LANES), F32)),
        grid=(bsz, seq // tile, hgn),
        in_specs=[
            pl.BlockSpec((1, tile, HEAD_DIM), lambda b, n, h: (b, n, lo + h)),
            pl.BlockSpec((1, tile, HEAD_DIM), lambda b, n, h: (b, n, N_HEADS + lo + h)),
            pl.BlockSpec((1, tile, HEAD_DIM), lambda b, n, h: (b, n, 2 * N_HEADS + lo + h)),
            pl.BlockSpec((1, prev_rows, HEAD_DIM), prev_map(N_HEADS)),
            pl.BlockSpec((1, prev_rows, HEAD_DIM), prev_map(2 * N_HEADS)),
        ],
        out_specs=(pl.BlockSpec((1, tile, HEAD_DIM), lambda b, n, h: (b, n, h)),
                   pl.BlockSpec((1, tile, LANES), lambda b, n, h: (b, n, 0))),
        compiler_params=_params("parallel", "parallel", "arbitrary"),
        name=f"attn_prompt_g{gi}",
    )(qkv, qkv, qkv, qkv, qkv)


def _kv_pack_kernel(*refs, hgn):
    o_ref = refs[-1]
    tp = refs[0].shape[1]
    for h in range(hgn):
        for kv in range(2):
            o_ref[0, pl.ds(2 * h + kv, tp, stride=2 * hgn), :] = refs[kv * hgn + h][0]


def kv_state_prompt(qkv, gi, *, keep):
    bsz, seq, _ = qkv.shape
    hgn = GROUP_HEADS[gi]
    lo = HEAD_OFFSETS[gi]
    tp = min(keep, 512)
    blk0 = (seq - keep) // tp
    specs = []
    for kv in range(2):
        for h in range(hgn):
            col = (1 + kv) * N_HEADS + lo + h
            specs.append(pl.BlockSpec((1, tp, HEAD_DIM), lambda b, p, col=col: (b, blk0 + p, col)))
    return pl.pallas_call(
        functools.partial(_kv_pack_kernel, hgn=hgn),
        out_shape=jax.ShapeDtypeStruct((bsz, keep * 2 * hgn, HEAD_DIM), F32),
        grid=(bsz, keep // tp),
        in_specs=specs,
        out_specs=pl.BlockSpec((1, tp * 2 * hgn, HEAD_DIM), lambda b, p: (b, p, 0)),
        compiler_params=_params("parallel", "parallel"),
        name=f"kv_state_prompt_g{gi}",
    )(*([qkv] * (2 * hgn)))


def _attn_sample_kernel(qkv_ref, c_ref, o_ref, lse_ref, st_ref, *, gi, t_new):
    dil = GROUP_DILATIONS[gi]
    window = GROUP_WINDOWS[gi]
    hgn = GROUP_HEADS[gi]
    lo = HEAD_OFFSETS[gi]
    per_pos = 2 * hgn
    rows = qkv_ref.shape[1]
    past = c_ref.shape[1] // per_pos
    scale = HEAD_DIM ** -0.5

    shift = t_new * per_pos
    kept = (past - t_new) * per_pos
    st_ref[0, 0:kept, :] = c_ref[0, shift:shift + kept, :]

    t_c = lax.broadcasted_iota(jnp.int32, (rows, past), 0)
    p_c = lax.broadcasted_iota(jnp.int32, (rows, past), 1)
    delta = past + t_c - p_c
    mask_c = jnp.logical_and(delta % dil == 0, delta <= window)
    t_n = lax.broadcasted_iota(jnp.int32, (rows, rows), 0)
    s_n = lax.broadcasted_iota(jnp.int32, (rows, rows), 1)
    dn = t_n - s_n
    mask_n = jnp.logical_and(jnp.logical_and(dn >= 0, dn % dil == 0),
                             jnp.logical_and(dn <= window, s_n < t_new))
    lane = lax.broadcasted_iota(jnp.int32, (rows, LANES), 1)
    lse_all = jnp.zeros((rows, LANES), F32)
    for h in range(hgn):
        def col(part):
            return slice((part * N_HEADS + lo + h) * HEAD_DIM, (part * N_HEADS + lo + h + 1) * HEAD_DIM)

        q = qkv_ref[0, :, col(0)]
        kn = qkv_ref[0, :, col(1)]
        vn = qkv_ref[0, :, col(2)]
        for t in range(t_new):
            row = kept + t * per_pos + 2 * h
            st_ref[0, row:row + 1, :] = kn[t:t + 1]
            st_ref[0, row + 1:row + 2, :] = vn[t:t + 1]
        kc = c_ref[0, pl.ds(2 * h, past, stride=per_pos), :]
        vc = c_ref[0, pl.ds(2 * h + 1, past, stride=per_pos), :]
        s_c = jnp.where(mask_c, _dot_nt(q, kc, F32) * scale, NEG_INF)
        s_new = jnp.where(mask_n, _dot_nt(q, kn, F32) * scale, NEG_INF)
        p_c, p_n, lse = _softmax_two_blocks(s_c, s_new, F32)
        o_ref[0, :, h * HEAD_DIM:(h + 1) * HEAD_DIM] = _dot(p_c, vc, F32) + _dot(p_n, vn, F32)
        lse_all = jnp.where(lane == h, lse, lse_all)
    lse_ref[0] = lse_all


def attn_sample_group(qkv_pad, cache, gi, *, t_new):
    dbs, rows, width = qkv_pad.shape
    crow = cache.shape[1]
    hgn = GROUP_HEADS[gi]
    assert (t_new * 2 * hgn) % SUBLANES == 0
    return pl.pallas_call(
        functools.partial(_attn_sample_kernel, gi=gi, t_new=t_new),
        out_shape=(jax.ShapeDtypeStruct((dbs, rows, hgn * HEAD_DIM), F32),
                   jax.ShapeDtypeStruct((dbs, rows, LANES), F32),
                   jax.ShapeDtypeStruct(cache.shape, F32)),
        grid=(dbs,),
        in_specs=[
            pl.BlockSpec((1, rows, width), lambda b: (b, 0, 0)),
            pl.BlockSpec((1, crow, HEAD_DIM), lambda b: (b, 0, 0)),
        ],
        out_specs=(pl.BlockSpec((1, rows, hgn * HEAD_DIM), lambda b: (b, 0, 0)),
                   pl.BlockSpec((1, rows, LANES), lambda b: (b, 0, 0)),
                   pl.BlockSpec((1, crow, HEAD_DIM), lambda b: (b, 0, 0))),
        compiler_params=_params("parallel"),
        name=f"attn_sample_g{gi}",
    )(qkv_pad, cache)


def _merge_kernel(o0_ref, o1_ref, o2_ref, l0_ref, l1_ref, l2_ref, m_ref):
    o_refs = (o0_ref, o1_ref, o2_ref)
    l_refs = (l0_ref, l1_ref, l2_ref)
    log_den = []
    for gi in range(N_GROUPS):
        lse = l_refs[gi][...]
        lane = lax.broadcasted_iota(jnp.int32, lse.shape, 1)
        valid = lane < GROUP_HEADS[gi]
        mx = jnp.max(jnp.where(valid, lse, -jnp.inf), axis=-1, keepdims=True)
        ssum = jnp.sum(jnp.where(valid, jnp.exp(lse - mx), 0.0), axis=-1, keepdims=True)
        log_den.append(mx + jnp.log(ssum) - math.log(GROUP_HEADS[gi]))
    top = jnp.maximum(jnp.maximum(log_den[0], log_den[1]), log_den[2])
    ex = [jnp.exp(ld - top) for ld in log_den]
    tot = ex[0] + ex[1] + ex[2]
    for gi in range(N_GROUPS):
        alpha = N_GROUPS * (ex[gi] / tot)
        cols = slice(HEAD_OFFSETS[gi] * HEAD_DIM, HEAD_OFFSETS[gi + 1] * HEAD_DIM)
        m_ref[:, cols] = (o_refs[gi][...] * alpha).astype(m_ref.dtype)


def merge_groups(outs, lses, *, tm, out_dtype=BF16):
    m = outs[0].shape[0]
    in_specs = [pl.BlockSpec((tm, o.shape[1]), lambda i: (i, 0)) for o in outs]
    in_specs += [pl.BlockSpec((tm, LANES), lambda i: (i, 0)) for _ in lses]
    return pl.pallas_call(
        _merge_kernel,
        out_shape=jax.ShapeDtypeStruct((m, N_HEADS * HEAD_DIM), out_dtype),
        grid=(m // tm,),
        in_specs=in_specs,
        out_specs=pl.BlockSpec((tm, N_HEADS * HEAD_DIM), lambda i: (i, 0)),
        compiler_params=_params("parallel"),
        name="merge_groups",
    )(*outs, *lses)


def _router_kernel(x_ref, g_ref, rw_ref, o_ref, *, mxu):
    logits = _dot(_rmsnorm_rows(x_ref[...], g_ref[...]), rw_ref[...], mxu)
    lane = lax.broadcasted_iota(jnp.int32, logits.shape, 1)
    valid = lane < N_EXPERTS
    mx = jnp.max(jnp.where(valid, logits, -jnp.inf), axis=-1, keepdims=True)
    ex = jnp.where(valid, jnp.exp(logits - mx), 0.0)
    probs = ex / jnp.sum(ex, axis=-1, keepdims=True)
    probs = jnp.where(valid, probs, -1.0)
    p1 = jnp.max(probs, axis=-1, keepdims=True)
    e1 = jnp.min(jnp.where(probs == p1, lane, LANES), axis=-1, keepdims=True)
    rest = jnp.where(lane == e1, -1.0, probs)
    p2 = jnp.max(rest, axis=-1, keepdims=True)
    e2 = jnp.min(jnp.where(rest == p2, lane, LANES), axis=-1, keepdims=True)
    tot = p1 + p2
    out = jnp.where(lane == 0, p1 / tot, 0.0)
    out = jnp.where(lane == 1, p2 / tot, out)
    out = jnp.where(lane == 2, e1.astype(F32), out)
    out = jnp.where(lane == 3, e2.astype(F32), out)
    o_ref[...] = out


def router(x, gain, router_w_pad, *, rows, tm, row_block0=0, mxu=BF16):
    d = x.shape[1]
    return pl.pallas_call(
        functools.partial(_router_kernel, mxu=mxu),
        out_shape=jax.ShapeDtypeStruct((rows, LANES), F32),
        grid=(rows // tm,),
        in_specs=[
            pl.BlockSpec((tm, d), lambda i: (row_block0 + i, 0)),
            pl.BlockSpec((1, d), lambda i: (0, 0)),
            pl.BlockSpec((d, LANES), lambda i: (0, 0)),
        ],
        out_specs=pl.BlockSpec((tm, LANES), lambda i: (i, 0)),
        compiler_params=_params("parallel"),
        name="router",
    )(x, gain.reshape(1, d), router_w_pad)


def _moe_kernel(tile_e_ref, tile_rows_ref, tile_pos_ref, tok_ref, x_hbm, g_ref,
                wg_ref, wu_ref, wd_ref, o_ref, x_scr, wg_scr, wu_scr, wd_scr, a_scr, sem, *, tm, sub):
    del tile_e_ref
    i = pl.program_id(0)
    j = pl.program_id(1)
    rows = tile_rows_ref[i]
    n_sub = (rows + sub - 1) // sub

    def row_copy(tok, r):
        return pltpu.make_async_copy(x_hbm.at[pl.ds(tok, 1)], o_ref.at[pl.ds(r, 1)], sem)

    @pl.when(jnp.logical_and(rows == 0, j == 0))
    def _():
        o_ref[...] = jnp.zeros_like(o_ref)

    @pl.when(rows > 0)
    def _():
        @pl.when(j == 0)
        def _():
            pos0 = tile_pos_ref[i]

            def issue(c, carry):
                for u in range(DMA_UNROLL):
                    r = c * DMA_UNROLL + u
                    row_copy(tok_ref[pos0 + r], r).start(priority=u % 2)
                return carry

            lax.fori_loop(0, n_sub * (sub // DMA_UNROLL), issue, 0)

            def wait(c, carry):
                for u in range(DMA_UNROLL):
                    row_copy(0, 0).wait()
                return carry

            lax.fori_loop(0, n_sub * (sub // DMA_UNROLL), wait, 0)
            for s in range(tm // sub):
                rs = slice(s * sub, (s + 1) * sub)

                @pl.when(s < n_sub)
                def _():
                    x_scr[rs, :] = _rmsnorm_rows(o_ref[rs, :], g_ref[...]).astype(BF16)

                o_ref[rs, :] = jnp.zeros((sub, o_ref.shape[1]), F32)

        def gate_up(s, cast_weights=False):
            x = x_scr[s * sub:(s + 1) * sub, :]
            if cast_weights:
                wg_scr[...] = wg_ref[0].astype(BF16)
            g = _dot(x, wg_scr[...])
            if cast_weights:
                wu_scr[...] = wu_ref[0].astype(BF16)
            u = _dot(x, wu_scr[...])
            a_scr[s % 2] = (g * jax.nn.sigmoid(g) * u).astype(BF16)

        def down(s):
            o_ref[s * sub:(s + 1) * sub, :] += _dot(a_scr[s % 2], wd_scr[...])

        gate_up(0, cast_weights=True)
        wd_scr[...] = wd_ref[0].astype(BF16)
        for s in range(1, tm // sub + 1):
            if s < tm // sub:
                @pl.when(s < n_sub)
                def _():
                    gate_up(s)
                    down(s - 1)

            @pl.when(s == n_sub)
            def _():
                down(s - 1)


def moe_experts(tile_e, tile_rows, tile_pos, tok_sorted, x, gain, w_gu, w_dn, *, tm, tf, sub):
    n_tiles = tile_e.shape[0]
    d = x.shape[1]
    f = w_dn.shape[1]
    nf = f // tf
    last = nf - 1

    def fchunk(i, j, rows):
        return jnp.where(rows[i] > 0, j, last)

    grid_spec = pltpu.PrefetchScalarGridSpec(
        num_scalar_prefetch=4,
        grid=(n_tiles, nf),
        in_specs=[
            pl.BlockSpec(memory_space=pl.ANY),
            pl.BlockSpec((1, d), lambda i, j, te, tr, tp, tk: (0, 0)),
            pl.BlockSpec((1, d, tf), lambda i, j, te, tr, tp, tk: (te[i], 0, fchunk(i, j, tr))),
            pl.BlockSpec((1, d, tf), lambda i, j, te, tr, tp, tk: (te[i], 0, nf + fchunk(i, j, tr))),
            pl.BlockSpec((1, tf, d), lambda i, j, te, tr, tp, tk: (te[i], fchunk(i, j, tr), 0)),
        ],
        out_specs=pl.BlockSpec((tm, d), lambda i, j, te, tr, tp, tk: (i, 0), pipeline_mode=pl.Buffered(1)),
        scratch_shapes=[
            pltpu.VMEM((tm, d), BF16),
            pltpu.VMEM((d, tf), BF16),
            pltpu.VMEM((d, tf), BF16),
            pltpu.VMEM((tf, d), BF16),
            pltpu.VMEM((2, sub, tf), BF16),
            pltpu.SemaphoreType.DMA,
        ],
    )
    return pl.pallas_call(
        functools.partial(_moe_kernel, tm=tm, sub=sub),
        out_shape=jax.ShapeDtypeStruct((n_tiles * tm, d), F32),
        grid_spec=grid_spec,
        compiler_params=_params("arbitrary", "arbitrary"),
        name="moe_experts",
    )(tile_e, tile_rows, tile_pos, tok_sorted, x, gain.reshape(1, d), w_gu, w_gu, w_dn)


def _combine_kernel(slot_ref, ys_hbm, x_ref, gate_ref, o_ref, buf, sems, *, tm, tok0):
    i = pl.program_id(0)
    n_steps = pl.num_programs(0)
    per_iter = max(1, DMA_UNROLL // TOP_K)

    def row_copy(slot, half, k, r):
        return pltpu.make_async_copy(ys_hbm.at[pl.ds(slot, 1)], buf.at[half, k, pl.ds(r, 1)], sems.at[half])

    def start_gather(step, half):
        def issue(c, carry):
            for u in range(per_iter):
                r = c * per_iter + u
                for k in range(TOP_K):
                    row_copy(slot_ref[(tok0 + step * tm + r) * TOP_K + k], half, k, r).start(priority=k % 2)
            return carry

        lax.fori_loop(0, tm // per_iter, issue, 0)

    def wait_gather(half):
        def wait(c, carry):
            for u in range(per_iter * TOP_K):
                row_copy(0, half, 0, 0).wait()
            return carry

        lax.fori_loop(0, tm // per_iter, wait, 0)

    half = i % 2

    @pl.when(i == 0)
    def _():
        start_gather(0, 0)

    @pl.when(i + 1 < n_steps)
    def _():
        start_gather(i + 1, 1 - half)

    wait_gather(half)
    gates = gate_ref[...]
    o_ref[...] = x_ref[...] + gates[:, 0:1] * buf[half, 0] + gates[:, 1:2] * buf[half, 1]


def moe_combine(slots_flat, ys, x, route, *, rows, tm, row_block0=0):
    d = x.shape[1]
    grid_spec = pltpu.PrefetchScalarGridSpec(
        num_scalar_prefetch=1,
        grid=(rows // tm,),
        in_specs=[
            pl.BlockSpec(memory_space=pl.ANY),
            pl.BlockSpec((tm, d), lambda i, s: (row_block0 + i, 0)),
            pl.BlockSpec((tm, LANES), lambda i, s: (i, 0)),
        ],
        out_specs=pl.BlockSpec((tm, d), lambda i, s: (i, 0)),
        scratch_shapes=[pltpu.VMEM((2, TOP_K, tm, d), F32), pltpu.SemaphoreType.DMA((2,))],
    )
    return pl.pallas_call(
        functools.partial(_combine_kernel, tm=tm, tok0=row_block0 * tm),
        out_shape=jax.ShapeDtypeStruct((rows, d), F32),
        grid_spec=grid_spec,
        compiler_params=_params("arbitrary"),
        name="moe_combine",
    )(slots_flat, ys, x, route)


def _routing_tables(route, *, tm, n_tiles):
    expert = route[:, 2:4].astype(jnp.int32)
    onehot = expert[:, :, None] == jnp.arange(N_EXPERTS, dtype=jnp.int32)[None, None, :]
    chosen = jnp.any(onehot, axis=1).astype(jnp.int32)
    before = jnp.cumsum(chosen, axis=0) - chosen
    counts = jnp.sum(chosen, axis=0)
    tiles_e = (counts + tm - 1) // tm
    tile_end = jnp.cumsum(tiles_e)
    tile_start = tile_end - tiles_e
    slot_te = tile_start[None, :] * tm + before
    slot_of_assign = jnp.sum(jnp.where(onehot, slot_te[:, None, :], 0), axis=-1).reshape(-1)
    order = jnp.argsort(expert.reshape(-1), stable=True).astype(jnp.int32)
    tok_sorted = jnp.pad(order // TOP_K, (0, tm))
    sorted_off = jnp.cumsum(counts) - counts
    tiles = jnp.arange(n_tiles, dtype=jnp.int32)
    n_active = tile_end[-1]
    te = jnp.minimum(jnp.searchsorted(tile_end, tiles, side="right"), N_EXPERTS - 1).astype(jnp.int32)
    local = (tiles - tile_start[te]) * tm
    active = tiles < n_active
    rows = jnp.where(active, jnp.clip(counts[te] - local, 0, tm), 0).astype(jnp.int32)
    pos = jnp.where(active, sorted_off[te] + local, 0).astype(jnp.int32)
    te = jnp.where(active, te, te[jnp.maximum(n_active - 1, 0)]).astype(jnp.int32)
    return te, rows, pos, tok_sorted, slot_of_assign.astype(jnp.int32)


def _row_tile(m, target):
    return target if m % target == 0 else m


def _cache_rows(cache):
    dbs, past, _, hgn, hd = cache.shape
    return jnp.transpose(cache, (0, 1, 3, 2, 4)).reshape(dbs, past * hgn * 2, hd)


def _state_from_rows(rows, hgn):
    bsz, n, hd = rows.shape
    past = n // (2 * hgn)
    return jnp.transpose(rows.reshape(bsz, past, hgn, 2, hd), (0, 1, 3, 2, 4))[None]


def kernel(x_prompt, x_sample, state_conv, cache_kv_w128, cache_kv_w512, cache_kv_w2048, norm_mix_w, norm_ffn_w, conv_in_w, conv_w, conv_out_w, attn_qkv_w, q_norm_w, k_norm_w, attn_out_w, ffn_gate_up_w, ffn_down_w, router_w, moe_gate_up_w, moe_down_w):
    bsz, seq, d = x_prompt.shape
    dbs, t_new, _ = x_sample.shape
    mp = bsz * seq
    ms = dbs * t_new
    xp = x_prompt.reshape(mp, d)
    xs = x_sample.reshape(ms, d)
    caches = (cache_kv_w128, cache_kv_w512, cache_kv_w2048)
    for gi in range(N_GROUPS):
        assert caches[gi].shape[2] == GROUP_WINDOWS[gi]
    assert CONV_WIDTH - 1 <= t_new <= SUBLANES and mp % ms == 0

    tmp = _row_tile(mp, 1024)

    b_p, u_p = conv_in_proj(xp, norm_mix_w[0], conv_in_w[0], tm=tmp, tn=256)
    b_s, u_s = conv_in_proj(xs, norm_mix_w[0], conv_in_w[0], tm=ms, tn=512, mxu=F32)
    z_p = conv_prompt(b_p, u_p, conv_w[0], seq=seq, tm=256)
    st = state_conv[0]
    prev1 = jnp.pad(st[:, 1:2], ((0, 0), (0, t_new - 1), (0, 0))).reshape(ms, d)
    prev2 = jnp.pad(st, ((0, 0), (0, t_new - 2), (0, 0))).reshape(ms, d)
    z_s = conv_sample(b_s, u_s, prev1, prev2, conv_w[0], t_new=t_new, out_dtype=F32)
    xp = matmul_res(z_p, conv_out_w[0], xp, tm=tmp, tn=1024)
    xs = matmul_res(z_s, conv_out_w[0], xs, tm=ms, tn=512)
    new_conv_prompt = u_p.reshape(bsz, seq, d)[:, -(CONV_WIDTH - 1):][None]
    new_conv_sample = u_s.reshape(dbs, t_new, d)[:, -(CONV_WIDTH - 1):][None]

    a_p = ffn_up(xp, norm_ffn_w[0], ffn_gate_up_w[0], tm=tmp, tn=512)
    a_s = ffn_up(xs, norm_ffn_w[0], ffn_gate_up_w[0], tm=ms, tn=512, mxu=F32)
    xp = matmul_res(a_p, ffn_down_w[0], xp, tm=tmp, tn=256)
    xs = matmul_res(a_s, ffn_down_w[0], xs, tm=ms, tn=256)

    qkv_p = qkv_proj(xp, norm_mix_w[1], attn_qkv_w[0], q_norm_w[0], k_norm_w[0], tm=tmp, tn=512)
    qkv_s = qkv_proj(xs, norm_mix_w[1], attn_qkv_w[0], q_norm_w[0], k_norm_w[0], tm=ms, tn=512, mxu=F32)
    qkv_p3 = qkv_p.reshape(bsz, seq, 3 * d)
    qkv_s_pad = jnp.pad(qkv_s.reshape(dbs, t_new, 3 * d), ((0, 0), (0, SUBLANES - t_new), (0, 0)))

    outs_p, lses_p, outs_s, lses_s = [], [], [], []
    kv_prompt, kv_sample = [], []
    for gi in range(N_GROUPS):
        hgn = GROUP_HEADS[gi]
        o, lse = attn_prompt_group(qkv_p3, gi)
        outs_p.append(o.reshape(mp, hgn * HEAD_DIM))
        lses_p.append(lse.reshape(mp, LANES))
        o, lse, state = attn_sample_group(qkv_s_pad, _cache_rows(caches[gi][0]), gi, t_new=t_new)
        outs_s.append(o[:, :t_new].reshape(ms, hgn * HEAD_DIM))
        lses_s.append(lse[:, :t_new].reshape(ms, LANES))
        kv_sample.append(_state_from_rows(state, hgn))
        keep = min(GROUP_WINDOWS[gi], seq)
        kv_prompt.append(_state_from_rows(kv_state_prompt(qkv_p3, gi, keep=keep), hgn))

    merged_p = merge_groups(outs_p, lses_p, tm=_row_tile(mp, 512))
    merged_s = merge_groups(outs_s, lses_s, tm=ms, out_dtype=F32)
    x_all = matmul_res_joint(merged_p, merged_s, attn_out_w[0], xp, xs, tm=tmp, tn=1024)

    rw_pad = jnp.pad(router_w[0], ((0, 0), (0, LANES - N_EXPERTS)))
    tmr = _row_tile(mp, 512)
    route_p = router(x_all, norm_ffn_w[1], rw_pad, rows=mp, tm=tmr)
    route_s = router(x_all, norm_ffn_w[1], rw_pad, rows=ms, tm=ms, row_block0=mp // ms, mxu=F32)
    route = jnp.concatenate([route_p, route_s], axis=0)
    tm_moe = 2304
    n_tiles = ((mp + ms) * TOP_K) // tm_moe + N_EXPERTS
    tile_e, tile_rows, tile_pos, tok_sorted, slot_of_assign = _routing_tables(
        route, tm=tm_moe, n_tiles=n_tiles)
    ys = moe_experts(tile_e, tile_rows, tile_pos, tok_sorted, x_all, norm_ffn_w[1],
                     moe_gate_up_w[0], moe_down_w[0], tm=tm_moe, tf=256, sub=256)
    tmc = _row_tile(mp, 256)
    y_p = moe_combine(slot_of_assign, ys, x_all, route_p, rows=mp, tm=tmc)
    y_s = moe_combine(slot_of_assign, ys, x_all, route_s, rows=ms, tm=ms, row_block0=mp // ms)

    return (y_p.reshape(bsz, seq, d), y_s.reshape(dbs, t_new, d), new_conv_prompt,
            kv_prompt[0], kv_prompt[1], kv_prompt[2], new_conv_sample,
            kv_sample[0], kv_sample[1], kv_sample[2])
```
